```python
import jax, jax.numpy as jnp
from jax import lax
import numpy as np

D_MODEL = 2048
BATCH = 8
SEQ = 4096
DEPTH = 1
DEC_BATCH = 8
DEC_SEQ = 16
PAST_LEN = 2048

CHUNK = 64
HEAD_DIM = 64
N_HEADS_A = 16
N_HEADS_B = 16
N_KV_B = 2
GQA_R = N_HEADS_B // N_KV_B
D_A = N_HEADS_A * HEAD_DIM
D_B = N_HEADS_B * HEAD_DIM
D_KV_B = N_KV_B * HEAD_DIM
D_IN = 3 * D_A + D_B + 2 * D_KV_B
A_PREV_CHUNKS = 8
A_REACH = A_PREV_CHUNKS * CHUNK
REL_CLIP = 128
B_WINDOW = 128
B_PREV_CHUNKS = B_WINDOW // CHUNK
ROPE_THETA = 500000.0
ROPE_DIM = HEAD_DIM // 4
D_FF = -(-8 * D_MODEL // (3 * 256)) * 256
NEG_INF = -1e30
EPS = 1e-6

kernel_name = 'hymba_chunk_band_swa_sink_encoder_step'


def rmsnorm(x, g):
    xf = x.astype(jnp.float32)
    y = xf * lax.rsqrt(jnp.mean(xf * xf, axis=-1, keepdims=True) + EPS)
    return (y * g.astype(jnp.float32)).astype(x.dtype)


def rope_partial(x, pos):
    half = ROPE_DIM // 2
    inv_freq = ROPE_THETA ** (-jnp.arange(half, dtype=jnp.float32) * 2.0 / ROPE_DIM)
    ang = pos.astype(jnp.float32)[:, None] * inv_freq[None, :]
    cos = jnp.cos(ang)[:, None, :]
    sin = jnp.sin(ang)[:, None, :]
    xf = x.astype(jnp.float32)
    x1 = xf[..., :half]
    x2 = xf[..., half:ROPE_DIM]
    out = jnp.concatenate([x1 * cos - x2 * sin, x2 * cos + x1 * sin, xf[..., ROPE_DIM:]], axis=-1)
    return out.astype(x.dtype)


def split_proj(p):
    b, s, _ = p.shape
    cuts = [D_A, 2 * D_A, 3 * D_A, 3 * D_A + D_B, 3 * D_A + D_B + D_KV_B]
    qa, ka, va, qb, kb, vb = jnp.split(p, cuts, axis=-1)
    return (qa.reshape(b, s, N_HEADS_A, HEAD_DIM), ka.reshape(b, s, N_HEADS_A, HEAD_DIM),
            va.reshape(b, s, N_HEADS_A, HEAD_DIM), qb.reshape(b, s, N_HEADS_B, HEAD_DIM),
            kb.reshape(b, s, N_KV_B, HEAD_DIM), vb.reshape(b, s, N_KV_B, HEAD_DIM))


def band_gather(t, n_prev):
    b, s, h, d = t.shape
    nc = s // CHUNK
    tp = jnp.pad(t.reshape(b, nc, CHUNK, h, d), ((0, 0), (n_prev, 0), (0, 0), (0, 0), (0, 0)))
    idx = jnp.arange(nc)[:, None] + jnp.arange(n_prev + 1)[None, :]
    return tp[:, idx].reshape(b, nc, (n_prev + 1) * CHUNK, h, d)


def band_valid(nc, n_prev):
    src = jnp.arange(nc)[:, None] - n_prev + jnp.arange(n_prev + 1)[None, :]
    return jnp.repeat(src >= 0, CHUNK, axis=1)


def rel_bias(table, qpos, kpos):
    rel = jnp.clip(qpos[:, None] - kpos[None, :], -REL_CLIP, REL_CLIP) + REL_CLIP
    return table.astype(jnp.float32)[:, rel][:, None]


def band_attend(q, k, v, valid, bias=None, sink=None):
    s = jnp.einsum('bgqhrd,bgkhd->bghrqk', q, k, preferred_element_type=jnp.float32) * (HEAD_DIM ** -0.5)
    if bias is not None:
        s = s + bias
    s = jnp.where(valid[None, :, None, None, None, :], s, NEG_INF)
    m = jnp.max(s, axis=-1, keepdims=True)
    if sink is not None:
        sk = sink.astype(jnp.float32)[None, None, :, :, None, None]
        m = jnp.maximum(m, sk)
        p = jnp.exp(s - m)
        denom = jnp.sum(p, axis=-1, keepdims=True) + jnp.exp(sk - m)
    else:
        p = jnp.exp(s - m)
        denom = jnp.sum(p, axis=-1, keepdims=True)
    w = (p / denom).astype(v.dtype)
    o = jnp.einsum('bghrqk,bgkhd->bgqhrd', w, v)
    b, g, nq = o.shape[:3]
    return o.reshape(b, g * nq, -1)


def merge_and_ffn(x, oa, ob, norm_grp_a, norm_grp_b, w_out, norm_ffn, w_gate, w_up, w_down):
    o = jnp.concatenate([rmsnorm(oa, norm_grp_a), rmsnorm(ob, norm_grp_b)], axis=-1)
    x = x + o @ w_out
    h = rmsnorm(x, norm_ffn)
    return x + (jax.nn.silu(h @ w_gate) * (h @ w_up)) @ w_down


def prompt_layer(x, w_in, norm_mix, rel_table, sinks, norm_grp_a, norm_grp_b, w_out,
                 norm_ffn, w_gate, w_up, w_down):
    b, s, _ = x.shape
    nc = s // CHUNK
    pos = jnp.arange(s, dtype=jnp.int32)
    qa, ka, va, qb, kb, vb = split_proj(rmsnorm(x, norm_mix) @ w_in)
    qb = rope_partial(qb, pos)
    kb = rope_partial(kb, pos)
    bias = rel_bias(rel_table, jnp.arange(CHUNK),
                    jnp.arange((A_PREV_CHUNKS + 1) * CHUNK) - A_PREV_CHUNKS * CHUNK)
    oa = band_attend(qa.reshape(b, nc, CHUNK, N_HEADS_A, 1, HEAD_DIM),
                     band_gather(ka, A_PREV_CHUNKS), band_gather(va, A_PREV_CHUNKS),
                     band_valid(nc, A_PREV_CHUNKS), bias=bias)
    ob = band_attend(qb.reshape(b, nc, CHUNK, N_KV_B, GQA_R, HEAD_DIM),
                     band_gather(kb, B_PREV_CHUNKS), band_gather(vb, B_PREV_CHUNKS),
                     band_valid(nc, B_PREV_CHUNKS), sink=sinks.reshape(N_KV_B, GQA_R))
    y = merge_and_ffn(x, oa, ob, norm_grp_a, norm_grp_b, w_out, norm_ffn, w_gate, w_up, w_down)
    keep_a = min(A_REACH, s)
    keep_b = min(B_WINDOW, s)
    return y, ka[:, s - keep_a:], va[:, s - keep_a:], kb[:, s - keep_b:], vb[:, s - keep_b:]


def sample_layer(x, ck_a, cv_a, ck_b, cv_b, w_in, norm_mix, rel_table, sinks, norm_grp_a,
                 norm_grp_b, w_out, norm_ffn, w_gate, w_up, w_down):
    b, s, _ = x.shape
    pos = PAST_LEN + jnp.arange(s, dtype=jnp.int32)
    qa, ka, va, qb, kb, vb = split_proj(rmsnorm(x, norm_mix) @ w_in)
    qb = rope_partial(qb, pos)
    kb = rope_partial(kb, pos)
    keep_a = ck_a.shape[1]
    keep_b = ck_b.shape[1]
    bias = rel_bias(rel_table, jnp.arange(s),
                    jnp.concatenate([jnp.arange(keep_a) - keep_a, jnp.arange(s)]))
    oa = band_attend(qa.reshape(b, 1, s, N_HEADS_A, 1, HEAD_DIM),
                     jnp.concatenate([ck_a, ka], axis=1)[:, None],
                     jnp.concatenate([cv_a, va], axis=1)[:, None],
                     jnp.ones((1, keep_a + s), dtype=bool), bias=bias)
    ob = band_attend(qb.reshape(b, 1, s, N_KV_B, GQA_R, HEAD_DIM),
                     jnp.concatenate([ck_b, kb], axis=1)[:, None],
                     jnp.concatenate([cv_b, vb], axis=1)[:, None],
                     jnp.ones((1, keep_b + s), dtype=bool), sink=sinks.reshape(N_KV_B, GQA_R))
    y = merge_and_ffn(x, oa, ob, norm_grp_a, norm_grp_b, w_out, norm_ffn, w_gate, w_up, w_down)
    return y, ka, va, kb, vb


def setup_inputs(seed: int = 0) -> dict:
    key = jax.random.key(seed)
    ks = jax.random.split(key, 20)
    f32 = jnp.float32
    keep_a = min(A_REACH, PAST_LEN)
    keep_b = min(B_WINDOW, PAST_LEN)

    def nrm(k, shape, scale):
        return jax.random.normal(k, shape, f32) * scale

    return {
        'x_prompt': nrm(ks[0], (BATCH, SEQ, D_MODEL), 1.0),
        'x_sample': nrm(ks[1], (DEC_BATCH, DEC_SEQ, D_MODEL), 1.0),
        'cache_a_k': nrm(ks[2], (DEPTH, DEC_BATCH, keep_a, N_HEADS_A, HEAD_DIM), 1.0),
        'cache_a_v': nrm(ks[3], (DEPTH, DEC_BATCH, keep_a, N_HEADS_A, HEAD_DIM), 1.0),
        'cache_b_k': nrm(ks[4], (DEPTH, DEC_BATCH, keep_b, N_KV_B, HEAD_DIM), 1.0),
        'cache_b_v': nrm(ks[5], (DEPTH, DEC_BATCH, keep_b, N_KV_B, HEAD_DIM), 1.0),
        'w_in': nrm(ks[6], (DEPTH, D_MODEL, D_IN), D_MODEL ** -0.5),
        'norm_mix': 1.0 + nrm(ks[7], (DEPTH, D_MODEL), 0.05),
        'rel_table': nrm(ks[8], (DEPTH, N_HEADS_A, 2 * REL_CLIP + 1), 0.1),
        'sinks': nrm(ks[9], (DEPTH, N_HEADS_B), 0.5),
        'norm_grp_a': 1.0 + nrm(ks[10], (DEPTH, D_A), 0.05),
        'norm_grp_b': 1.0 + nrm(ks[11], (DEPTH, D_B), 0.05),
        'w_out': nrm(ks[12], (DEPTH, D_A + D_B, D_MODEL), (D_A + D_B) ** -0.5),
        'norm_ffn': 1.0 + nrm(ks[13], (DEPTH, D_MODEL), 0.05),
        'w_gate': nrm(ks[14], (DEPTH, D_MODEL, D_FF), D_MODEL ** -0.5),
        'w_up': nrm(ks[15], (DEPTH, D_MODEL, D_FF), D_MODEL ** -0.5),
        'w_down': nrm(ks[16], (DEPTH, D_FF, D_MODEL), D_FF ** -0.5),
        'norm_final': 1.0 + nrm(ks[17], (D_MODEL,), 0.05),
    }


def reference(x_prompt, x_sample, cache_a_k, cache_a_v, cache_b_k, cache_b_v, w_in, norm_mix,
              rel_table, sinks, norm_grp_a, norm_grp_b, w_out, norm_ffn, w_gate, w_up, w_down,
              norm_final):
    xp = x_prompt
    xs = x_sample
    pa_k, pa_v, pb_k, pb_v = [], [], [], []
    sa_k, sa_v, sb_k, sb_v = [], [], [], []
    for l in range(DEPTH):
        w = (w_in[l], norm_mix[l], rel_table[l], sinks[l], norm_grp_a[l], norm_grp_b[l],
             w_out[l], norm_ffn[l], w_gate[l], w_up[l], w_down[l])
        xp, ak, av, bk, bv = prompt_layer(xp, *w)
        pa_k.append(ak); pa_v.append(av); pb_k.append(bk); pb_v.append(bv)
        xs, ak, av, bk, bv = sample_layer(xs, cache_a_k[l], cache_a_v[l], cache_b_k[l], cache_b_v[l], *w)
        sa_k.append(ak); sa_v.append(av); sb_k.append(bk); sb_v.append(bv)
    y_prompt = rmsnorm(xp, norm_final)
    y_sample = rmsnorm(xs, norm_final)
    return (y_prompt, y_sample,
            jnp.stack(pa_k), jnp.stack(pa_v), jnp.stack(pb_k), jnp.stack(pb_v),
            jnp.stack(sa_k), jnp.stack(sa_v), jnp.stack(sb_k), jnp.stack(sb_v))
```

```python
import functools

import jax
import jax.numpy as jnp
from jax import lax
from jax.experimental import pallas as pl
from jax.experimental.pallas import tpu as pltpu

D_MODEL = 2048
CHUNK = 64
HEAD_DIM = 64
N_HEADS_A = 16
N_HEADS_B = 16
N_KV_B = 2
GQA_R = N_HEADS_B // N_KV_B
D_A = N_HEADS_A * HEAD_DIM
D_B = N_HEADS_B * HEAD_DIM
D_KV_B = N_KV_B * HEAD_DIM
D_IN = 3 * D_A + D_B + 2 * D_KV_B
A_PREV_CHUNKS = 8
A_REACH = A_PREV_CHUNKS * CHUNK
REL_CLIP = 128
B_WINDOW = 128
B_PREV_CHUNKS = B_WINDOW // CHUNK
ROPE_THETA = 500000.0
ROPE_DIM = HEAD_DIM // 4
D_FF = 5632
NEG_INF = -1e30
EPS = 1e-6
PAST_LEN = 2048

LANES = 128
QUAD = 4 * HEAD_DIM
N_QUADS = N_HEADS_A // 4
N_PAIRS = N_HEADS_B // 2
TQ = 512
A_WIN = (A_PREV_CHUNKS + 1) * CHUNK
A_WIN_PAD = 640
B_WIN = (B_PREV_CHUNKS + 1) * CHUNK
B_WIN_PAD = 256
TF = 512
VMEM_LIMIT = 56 * 1024 * 1024

F32 = jnp.float32
BF16 = jnp.bfloat16

_B_PERM = [half * GQA_R + p for p in range(N_PAIRS) for half in range(N_KV_B)]


def _cparams(sem):
    return pltpu.CompilerParams(dimension_semantics=sem, vmem_limit_bytes=VMEM_LIMIT)


def _proj_body(x_ref, g_ref, w_ref, cos_ref, sa_ref, sb_ref,
               qa_ref, ka_ref, va_ref, qb_ref, kb_ref, vb_ref,
               kat_ref, vat_ref, kbt_ref, vbt_ref, *, tm, keep_a, keep_b):
    x = x_ref[...]
    ms = jnp.mean(x * x, axis=-1, keepdims=True)
    xn = ((x * lax.rsqrt(ms + EPS)) * g_ref[...]).astype(BF16)
    last = pl.program_id(1) == pl.num_programs(1) - 1

    def proj(lo, hi):
        return jnp.dot(xn, w_ref[:, lo:hi], preferred_element_type=F32)

    cos = cos_ref[...]
    sa = sa_ref[...]
    sb = sb_ref[...]

    def rope(y):
        return y * cos + pltpu.roll(y, LANES - ROPE_DIM // 2, 1) * sa + pltpu.roll(y, ROPE_DIM // 2, 1) * sb

    qa_ref[...] = proj(0, D_A).astype(BF16)

    ka = proj(D_A, 2 * D_A)
    ka_ref[...] = ka.astype(BF16)

    @pl.when(last)
    def _():
        kat_ref[...] = ka[tm - keep_a:, :]

    va = proj(2 * D_A, 3 * D_A)
    va_ref[...] = va.astype(BF16)

    @pl.when(last)
    def _():
        vat_ref[...] = va[tm - keep_a:, :]

    qb = proj(3 * D_A, 3 * D_A + D_B)
    for j in range(D_B // LANES):
        qb_ref[:, j * LANES:(j + 1) * LANES] = rope(qb[:, j * LANES:(j + 1) * LANES]).astype(BF16)

    kvb = proj(3 * D_A + D_B, D_IN)
    kb = rope(kvb[:, :D_KV_B])
    vb = kvb[:, D_KV_B:]
    kb_ref[...] = kb.astype(BF16)
    vb_ref[...] = vb.astype(BF16)

    @pl.when(last)
    def _():
        kbt_ref[...] = kb[tm - keep_b:, :]
        vbt_ref[...] = vb[tm - keep_b:, :]


def _proj_call(x3, g_mix, w_p, cos_t, sa_t, sb_t, *, tm, keep_a, keep_b):
    nb, s, d = x3.shape
    assert s % tm == 0 and keep_a <= tm and keep_b <= tm
    grid = (nb, s // tm)

    def tok(width):
        return jax.ShapeDtypeStruct((nb, s, width), BF16)

    out_shape = ([tok(D_A)] * 4 + [tok(D_KV_B)] * 2
                 + [jax.ShapeDtypeStruct((nb, keep_a, D_A), F32)] * 2
                 + [jax.ShapeDtypeStruct((nb, keep_b, D_KV_B), F32)] * 2)
    tok_spec = lambda width: pl.BlockSpec((None, tm, width), lambda b, i: (b, i, 0))
    tail_spec = lambda rows, width: pl.BlockSpec((None, rows, width), lambda b, i: (b, 0, 0))
    tab_spec = pl.BlockSpec((tm, LANES), lambda b, i: (i, 0))
    in_specs = [
        tok_spec(d),
        pl.BlockSpec((1, d), lambda b, i: (0, 0)),
        pl.BlockSpec((d, D_IN), lambda b, i: (0, 0), pipeline_mode=pl.Buffered(1)),
        tab_spec, tab_spec, tab_spec,
    ]
    out_specs = ([tok_spec(D_A)] * 4 + [tok_spec(D_KV_B)] * 2
                 + [tail_spec(keep_a, D_A)] * 2 + [tail_spec(keep_b, D_KV_B)] * 2)
    return pl.pallas_call(
        functools.partial(_proj_body, tm=tm, keep_a=keep_a, keep_b=keep_b),
        grid=grid, in_specs=in_specs, out_specs=out_specs, out_shape=out_shape,
        compiler_params=_cparams(("arbitrary", "arbitrary")),
        name="proj",
    )(x3, g_mix, w_p, cos_t, sa_t, sb_t)


def _attend_a_quad(q, kw, vw, bias, hm_ref, thr):
    r = q.shape[0]
    qs = jnp.concatenate([q * hm_ref[h] for h in range(4)], axis=0)
    s = lax.dot_general(qs, kw, (((1,), (1,)), ((), ())), preferred_element_type=F32) + bias
    if thr is not None:
        col = lax.broadcasted_iota(jnp.int32, s.shape, 1)
        s = jnp.where(col >= thr, s, NEG_INF)
    m = jnp.max(s, axis=-1, keepdims=True)
    p = jnp.exp(s - m)
    denom = jnp.sum(p, axis=-1, keepdims=True)
    pv = jnp.dot(p.astype(BF16), vw, preferred_element_type=F32) * (1.0 / denom)
    lane_head = lax.broadcasted_iota(jnp.int32, (r, QUAD), 1) // HEAD_DIM
    o = pv[0:r]
    for h in range(1, 4):
        o = jnp.where(lane_head == h, pv[h * r:(h + 1) * r], o)
    return o


def _attend_b(q, kw, vw, maskrow, sink, lm_ref, thr):
    r = q.shape[0]
    pieces = []
    for p in range(N_PAIRS):
        qp = q[:, p * LANES:(p + 1) * LANES]
        pieces.append(qp * lm_ref[0])
        pieces.append(qp * lm_ref[1])
    qs = jnp.concatenate(pieces, axis=0)
    s = lax.dot_general(qs, kw, (((1,), (1,)), ((), ())), preferred_element_type=F32) + maskrow
    if thr is not None:
        col = lax.broadcasted_iota(jnp.int32, s.shape, 1)
        s = jnp.where(col >= thr, s, NEG_INF)
    m = jnp.maximum(jnp.max(s, axis=-1, keepdims=True), sink)
    p = jnp.exp(s - m)
    denom = jnp.sum(p, axis=-1, keepdims=True) + jnp.exp(sink - m)
    pv = jnp.dot(p.astype(BF16), vw, preferred_element_type=F32) * (1.0 / denom)
    lower = lax.broadcasted_iota(jnp.int32, (r, LANES), 1) < HEAD_DIM
    outs = [jnp.where(lower, pv[(2 * p) * r:(2 * p + 1) * r], pv[(2 * p + 1) * r:(2 * p + 2) * r])
            for p in range(N_PAIRS)]
    return jnp.concatenate(outs, axis=1)


def _group_norm_store(o_parts, g_ref, o_ref, rows, col0):
    width = sum(o.shape[1] for o in o_parts)
    ssq = sum(jnp.sum(o * o, axis=-1, keepdims=True) for o in o_parts)
    inv = lax.rsqrt(ssq * (1.0 / width) + EPS)
    off = 0
    for o in o_parts:
        w = o.shape[1]
        o_ref[rows, col0 + off:col0 + off + w] = ((o * inv) * g_ref[:, off:off + w]).astype(BF16)
        off += w


def _attn_prompt_body(qa_ref, kap_ref, kac_ref, vap_ref, vac_ref,
                      qb_ref, kbp_ref, kbc_ref, vbp_ref, vbc_ref,
                      bias_ref, hm_ref, lm_ref, maskb_ref, sink_ref, ga_ref, gb_ref,
                      o_ref, kwin, vwin, kbwin, vbwin):
    i = pl.program_id(1)
    for win, prev, cur in ((kwin, kap_ref, kac_ref), (vwin, vap_ref, vac_ref),
                           (kbwin, kbp_ref, kbc_ref), (vbwin, vbp_ref, vbc_ref)):
        win[0:TQ] = prev[...]
        win[TQ:2 * TQ] = cur[...]
        win[2 * TQ:2 * TQ + CHUNK] = jnp.zeros((CHUNK, win.shape[1]), BF16)

    def run(first_step):
        def body(c, carry):
            r0 = pl.multiple_of(c * CHUNK, CHUNK)
            rows = pl.ds(r0, CHUNK)
            thr_a = (A_PREV_CHUNKS - c) * CHUNK if first_step else None
            thr_b = jnp.maximum(B_PREV_CHUNKS - c, 0) * CHUNK if first_step else None
            parts = []
            for quad in range(N_QUADS):
                cols = slice(quad * QUAD, (quad + 1) * QUAD)
                parts.append(_attend_a_quad(
                    qa_ref[rows, cols], kwin[pl.ds(r0, A_WIN_PAD), cols], vwin[pl.ds(r0, A_WIN_PAD), cols],
                    bias_ref[quad], hm_ref, thr_a))
            _group_norm_store(parts, ga_ref, o_ref, rows, 0)
            b0 = pl.multiple_of(c * CHUNK + (TQ - B_WINDOW), CHUNK)
            ob = _attend_b(qb_ref[rows, :], kbwin[pl.ds(b0, B_WIN_PAD), :], vbwin[pl.ds(b0, B_WIN_PAD), :],
                           maskb_ref[...], sink_ref[...], lm_ref, thr_b)
            _group_norm_store([ob], gb_ref, o_ref, rows, D_A)
            return carry
        lax.fori_loop(0, TQ // CHUNK, body, 0)

    @pl.when(i == 0)
    def _():
        run(True)

    @pl.when(i > 0)
    def _():
        run(False)


def _attn_prompt_call(qa, ka, va, qb, kb, vb, bias, hm, lm, maskb, sink_rows, g_a, g_b):
    nb, s, _ = qa.shape
    assert s % TQ == 0
    grid = (nb, s // TQ)
    cur = lambda width: pl.BlockSpec((None, TQ, width), lambda b, i: (b, i, 0))
    prev = lambda width: pl.BlockSpec((None, TQ, width), lambda b, i: (b, jnp.maximum(i - 1, 0), 0))
    const = lambda shape: pl.BlockSpec(shape, lambda b, i: (0,) * len(shape))
    in_specs = [
        cur(D_A), prev(D_A), cur(D_A), prev(D_A), cur(D_A),
        cur(D_B), prev(D_KV_B), cur(D_KV_B), prev(D_KV_B), cur(D_KV_B),
        const(bias.shape), const(hm.shape), const(lm.shape), const(maskb.shape), const(sink_rows.shape),
        const(g_a.shape), const(g_b.shape),
    ]
    win_rows = 2 * TQ + CHUNK
    return pl.pallas_call(
        _attn_prompt_body,
        grid=grid, in_specs=in_specs,
        out_specs=pl.BlockSpec((None, TQ, D_A + D_B), lambda b, i: (b, i, 0)),
        out_shape=jax.ShapeDtypeStruct((nb, s, D_A + D_B), BF16),
        scratch_shapes=[pltpu.VMEM((win_rows, D_A), BF16), pltpu.VMEM((win_rows, D_A), BF16),
                        pltpu.VMEM((win_rows, D_KV_B), BF16), pltpu.VMEM((win_rows, D_KV_B), BF16)],
        compiler_params=_cparams(("arbitrary", "arbitrary")),
        name="attn_prompt",
    )(qa, ka, ka, va, va, qb, kb, kb, vb, vb, bias, hm, lm, maskb, sink_rows, g_a, g_b)


def _attn_sample_body(qa_ref, cka_ref, cva_ref, nka_ref, nva_ref,
                      qb_ref, ckb_ref, cvb_ref, nkb_ref, nvb_ref,
                      bias_ref, hm_ref, lm_ref, maskb_ref, sink_ref, ga_ref, gb_ref, o_ref, *, pad_a, pad_b):
    rows = slice(None)
    parts = []
    for quad in range(N_QUADS):
        cols = slice(quad * QUAD, (quad + 1) * QUAD)
        kw = jnp.concatenate([cka_ref[:, cols].astype(BF16), nka_ref[:, cols],
                              jnp.zeros((pad_a, QUAD), BF16)], axis=0)
        vw = jnp.concatenate([cva_ref[:, cols].astype(BF16), nva_ref[:, cols],
                              jnp.zeros((pad_a, QUAD), BF16)], axis=0)
        parts.append(_attend_a_quad(qa_ref[:, cols], kw, vw, bias_ref[quad], hm_ref, None))
    _group_norm_store(parts, ga_ref, o_ref, rows, 0)
    kbw = jnp.concatenate([ckb_ref[...].astype(BF16), nkb_ref[...], jnp.zeros((pad_b, D_KV_B), BF16)], axis=0)
    vbw = jnp.concatenate([cvb_ref[...].astype(BF16), nvb_ref[...], jnp.zeros((pad_b, D_KV_B), BF16)], axis=0)
    ob = _attend_b(qb_ref[...], kbw, vbw, maskb_ref[...], sink_ref[...], lm_ref, None)
    _group_norm_store([ob], gb_ref, o_ref, rows, D_A)


def _attn_sample_call(qa, cka, cva, nka, nva, qb, ckb, cvb, nkb, nvb, bias, hm, lm, maskb, sink_rows, g_a, g_b,
                      *, pad_a, pad_b):
    nb, s, _ = qa.shape
    blk = lambda a: pl.BlockSpec((None,) + a.shape[1:], lambda b: (b, 0, 0))
    const = lambda a: pl.BlockSpec(a.shape, lambda b: (0,) * a.ndim)
    per_batch = (qa, cka, cva, nka, nva, qb, ckb, cvb, nkb, nvb)
    consts = (bias, hm, lm, maskb, sink_rows, g_a, g_b)
    return pl.pallas_call(
        functools.partial(_attn_sample_body, pad_a=pad_a, pad_b=pad_b),
        grid=(nb,),
        in_specs=[blk(a) for a in per_batch] + [const(a) for a in consts],
        out_specs=pl.BlockSpec((None, s, D_A + D_B), lambda b: (b, 0, 0)),
        out_shape=jax.ShapeDtypeStruct((nb, s, D_A + D_B), BF16),
        compiler_params=_cparams(("arbitrary",)),
        name="attn_sample",
    )(*per_batch, *consts)


def _ffn_body(x_ref, o_ref, wo_ref, gf_ref, wgu_ref, wd_ref, gfin_ref, y_ref, h_scr):
    k = pl.program_id(1)

    @pl.when(k == 0)
    def _():
        x1 = x_ref[...] + jnp.dot(o_ref[...], wo_ref[...], preferred_element_type=F32)
        y_ref[...] = x1
        ms = jnp.mean(x1 * x1, axis=-1, keepdims=True)
        h_scr[...] = ((x1 * lax.rsqrt(ms + EPS)) * gf_ref[...]).astype(BF16)

    gu = jnp.dot(h_scr[...], wgu_ref[...], preferred_element_type=F32)
    g = gu[:, :TF]
    u = gu[:, TF:]
    act = (g * jax.nn.sigmoid(g)) * u
    y_ref[...] += jnp.dot(act.astype(BF16), wd_ref[...], preferred_element_type=F32)

    @pl.when(k == pl.num_programs(1) - 1)
    def _():
        y = y_ref[...]
        ms = jnp.mean(y * y, axis=-1, keepdims=True)
        y_ref[...] = (y * lax.rsqrt(ms + EPS)) * gfin_ref[...]


def _ffn_call(x2, o2, w_out, g_ffn, w_gu, w_d, g_fin, *, tm):
    n, d = x2.shape
    assert n % tm == 0
    grid = (n // tm, D_FF // TF)
    in_specs = [
        pl.BlockSpec((tm, d), lambda i, k: (i, 0)),
        pl.BlockSpec((tm, D_A + D_B), lambda i, k: (i, 0)),
        pl.BlockSpec((D_A + D_B, d), lambda i, k: (0, 0), pipeline_mode=pl.Buffered(1)),
        pl.BlockSpec((1, d), lambda i, k: (0, 0)),
        pl.BlockSpec((d, 2 * TF), lambda i, k: (0, k)),
        pl.BlockSpec((TF, d), lambda i, k: (k, 0)),
        pl.BlockSpec((1, d), lambda i, k: (0, 0)),
    ]
    return pl.pallas_call(
        _ffn_body,
        grid=grid, in_specs=in_specs,
        out_specs=pl.BlockSpec((tm, d), lambda i, k: (i, 0)),
        out_shape=jax.ShapeDtypeStruct((n, d), F32),
        scratch_shapes=[pltpu.VMEM((tm, d), BF16)],
        compiler_params=_cparams(("arbitrary", "arbitrary")),
        name="ffn",
    )(x2, o2, w_out, g_ffn, w_gu, w_d, g_fin)


def _rope_tables(pos):
    half = ROPE_DIM // 2
    n = pos.shape[0]
    inv_freq = ROPE_THETA ** (-jnp.arange(half, dtype=F32) * 2.0 / ROPE_DIM)
    ang = pos.astype(F32)[:, None] * inv_freq[None, :]
    cos = jnp.cos(ang)
    sin = jnp.sin(ang)
    rest = HEAD_DIM - ROPE_DIM
    cos_h = jnp.concatenate([cos, cos, jnp.ones((n, rest), F32)], axis=1)
    sa_h = jnp.concatenate([-sin, jnp.zeros((n, HEAD_DIM - half), F32)], axis=1)
    sb_h = jnp.concatenate([jnp.zeros((n, half), F32), sin, jnp.zeros((n, rest), F32)], axis=1)
    rep = LANES // HEAD_DIM
    return tuple(jnp.tile(t, (1, rep)) for t in (cos_h, sa_h, sb_h))


def _bias_table(rel_table, qpos, kpos, width):
    rel = jnp.clip(qpos[:, None] - kpos[None, :], -REL_CLIP, REL_CLIP) + REL_CLIP
    b = rel_table.astype(F32)[:, rel]
    nq, nk = rel.shape
    b = jnp.pad(b, ((0, 0), (0, 0), (0, width - nk)), constant_values=NEG_INF)
    return b.reshape(N_QUADS, 4 * nq, width)


def _pad_mask_row(live, width):
    return jnp.concatenate([jnp.zeros((1, live), F32), jnp.full((1, width - live), NEG_INF, F32)], axis=1)


def _sink_rows(sinks, r):
    perm = jnp.asarray(_B_PERM)
    return jnp.repeat(sinks.astype(F32)[perm], r)[:, None]


def kernel(x_prompt, x_sample, cache_a_k, cache_a_v, cache_b_k, cache_b_v, w_in, norm_mix, rel_table, sinks,
           norm_grp_a, norm_grp_b, w_out, norm_ffn, w_gate, w_up, w_down, norm_final):
    assert w_in.shape[0] == 1, "single-layer problem"
    nb, s, d = x_prompt.shape
    sb, ss, _ = x_sample.shape
    keep_a = min(A_REACH, s)
    keep_b = min(B_WINDOW, s)
    perm = jnp.asarray(_B_PERM)
    scale = HEAD_DIM ** -0.5

    w = w_in[0]
    w_qb = w[:, 3 * D_A:3 * D_A + D_B].reshape(d, N_HEADS_B, HEAD_DIM)[:, perm].reshape(d, D_B)
    w_p = jnp.concatenate([w[:, :D_A] * scale, w[:, D_A:3 * D_A], w_qb * scale, w[:, 3 * D_A + D_B:]],
                          axis=1).astype(BF16)
    g_mix = norm_mix[0][None, :]
    g_a = norm_grp_a[0][None, :]
    g_b = norm_grp_b[0].reshape(N_HEADS_B, HEAD_DIM)[perm].reshape(1, D_B)
    wo = w_out[0]
    wo_b = wo[D_A:].reshape(N_HEADS_B, HEAD_DIM, d)[perm].reshape(D_B, d)
    wo_p = jnp.concatenate([wo[:D_A], wo_b], axis=0).astype(BF16)
    g_ffn = norm_ffn[0][None, :]
    nkf = D_FF // TF
    w_gu = jnp.concatenate([w_gate[0].reshape(d, nkf, 1, TF), w_up[0].reshape(d, nkf, 1, TF)],
                           axis=2).reshape(d, 2 * D_FF).astype(BF16)
    w_d = w_down[0].astype(BF16)
    g_fin = norm_final[None, :]

    def head_masks(width, rows):
        lane = jnp.arange(width) // HEAD_DIM
        m = (lane[None, :] == jnp.arange(width // HEAD_DIM)[:, None]).astype(BF16)
        return jnp.broadcast_to(m[:, None, :], (width // HEAD_DIM, rows, width))

    hm, lm = head_masks(QUAD, CHUNK), head_masks(LANES, CHUNK)
    hm_s, lm_s = head_masks(QUAD, ss), head_masks(LANES, ss)

    tables = _rope_tables(jnp.arange(s, dtype=jnp.int32))
    qa, ka, va, qb, kb, vb, ka_t, va_t, kb_t, vb_t = _proj_call(
        x_prompt, g_mix, w_p, *tables, tm=TQ, keep_a=keep_a, keep_b=keep_b)
    bias_p = _bias_table(rel_table[0], jnp.arange(CHUNK), jnp.arange(A_WIN) - A_REACH, A_WIN_PAD)
    o_p = _attn_prompt_call(qa, ka, va, qb, kb, vb, bias_p, hm, lm, _pad_mask_row(B_WIN, B_WIN_PAD),
                            _sink_rows(sinks[0], CHUNK), g_a, g_b)
    y_p = _ffn_call(x_prompt.reshape(nb * s, d), o_p.reshape(nb * s, D_A + D_B), wo_p, g_ffn, w_gu, w_d, g_fin,
                    tm=TQ).reshape(nb, s, d)

    ns = sb * ss
    pos_s = jnp.tile(PAST_LEN + jnp.arange(ss, dtype=jnp.int32), sb)
    tables_s = _rope_tables(pos_s)
    sqa, ska, sva, sqb, skb, svb, ska_t, sva_t, skb_t, svb_t = _proj_call(
        x_sample.reshape(1, ns, d), g_mix, w_p, *tables_s, tm=ns, keep_a=ns, keep_b=ns)
    ca_len = cache_a_k.shape[2]
    cb_len = cache_b_k.shape[2]
    live_a = ca_len + ss
    live_b = cb_len + ss
    wa = -(-live_a // LANES) * LANES
    wb = -(-live_b // LANES) * LANES
    bias_s = _bias_table(rel_table[0], jnp.arange(ss),
                         jnp.concatenate([jnp.arange(ca_len) - ca_len, jnp.arange(ss)]), wa)
    per_b = lambda a, width: a.reshape(sb, -1, width)
    o_s = _attn_sample_call(
        per_b(sqa, D_A), per_b(cache_a_k[0], D_A), per_b(cache_a_v[0], D_A), per_b(ska, D_A), per_b(sva, D_A),
        per_b(sqb, D_B), per_b(cache_b_k[0], D_KV_B), per_b(cache_b_v[0], D_KV_B), per_b(skb, D_KV_B),
        per_b(svb, D_KV_B),
        bias_s, hm_s, lm_s, _pad_mask_row(live_b, wb), _sink_rows(sinks[0], ss), g_a, g_b,
        pad_a=wa - live_a, pad_b=wb - live_b)
    y_s = _ffn_call(x_sample.reshape(ns, d), o_s.reshape(ns, D_A + D_B), wo_p, g_ffn, w_gu, w_d, g_fin,
                    tm=ns).reshape(sb, ss, d)

    heads_a = lambda t, nbat: t.reshape(1, nbat, -1, N_HEADS_A, HEAD_DIM)
    heads_b = lambda t, nbat: t.reshape(1, nbat, -1, N_KV_B, HEAD_DIM)
    return (y_p, y_s,
            heads_a(ka_t, nb), heads_a(va_t, nb), heads_b(kb_t, nb), heads_b(vb_t, nb),
            heads_a(ska_t, sb), heads_a(sva_t, sb), heads_b(skb_t, sb), heads_b(svb_t, sb))
```

```python
import functools

import numpy as np
import jax
import jax.numpy as jnp
from jax import lax
from jax.experimental import pallas as pl
from jax.experimental.pallas import tpu as pltpu

D_MODEL = 2048
CHUNK = 64
HEAD_DIM = 64
N_HEADS_A = 16
N_HEADS_B = 16
N_KV_B = 2
GQA_R = N_HEADS_B // N_KV_B
D_A = N_HEADS_A * HEAD_DIM
D_B = N_HEADS_B * HEAD_DIM
D_KV_B = N_KV_B * HEAD_DIM
D_IN = 3 * D_A + D_B + 2 * D_KV_B
A_PREV_CHUNKS = 8
A_REACH = A_PREV_CHUNKS * CHUNK
REL_CLIP = 128
B_WINDOW = 128
B_PREV_CHUNKS = B_WINDOW // CHUNK
ROPE_THETA = 500000.0
ROPE_DIM = HEAD_DIM // 4
D_FF = 5632
NEG_INF = -1e30
EPS = 1e-6
PAST_LEN = 2048

LANES = 128
QUAD = 4 * HEAD_DIM
N_QUADS = N_HEADS_A // 4
N_PAIRS = N_HEADS_B // 2
TQ = 512
A_WIN = (A_PREV_CHUNKS + 1) * CHUNK
A_WIN_PAD = 640
B_WIN = (B_PREV_CHUNKS + 1) * CHUNK
B_WIN_PAD = 256
TF = 512
VMEM_LIMIT = 56 * 1024 * 1024

F32 = jnp.float32
BF16 = jnp.bfloat16


def _cparams(sem):
    return pltpu.CompilerParams(dimension_semantics=sem, vmem_limit_bytes=VMEM_LIMIT)


def _proj_body(x_ref, g_ref, w_ref, cos_ref, sa_ref, sb_ref,
               qa_ref, ka_ref, va_ref, qb_ref, kb_ref, vb_ref,
               kat_ref, vat_ref, kbt_ref, vbt_ref, *, tm, keep_a, keep_b):
    x = x_ref[...]
    ms = jnp.mean(x * x, axis=-1, keepdims=True)
    xn = ((x * lax.rsqrt(ms + EPS)) * g_ref[...]).astype(BF16)
    last = pl.program_id(1) == pl.num_programs(1) - 1

    def proj(lo, hi):
        return jnp.dot(xn, w_ref[:, lo:hi], preferred_element_type=F32)

    cos = cos_ref[...]
    sa = sa_ref[...]
    sb = sb_ref[...]

    def rope(y):
        return y * cos + pltpu.roll(y, LANES - ROPE_DIM // 2, 1) * sa + pltpu.roll(y, ROPE_DIM // 2, 1) * sb

    qa_ref[...] = proj(0, D_A).astype(BF16)

    ka = proj(D_A, 2 * D_A)
    ka_ref[...] = ka.astype(BF16)

    @pl.when(last)
    def _():
        kat_ref[...] = ka[tm - keep_a:, :]

    va = proj(2 * D_A, 3 * D_A)
    va_ref[...] = va.astype(BF16)

    @pl.when(last)
    def _():
        vat_ref[...] = va[tm - keep_a:, :]

    qb = proj(3 * D_A, 3 * D_A + D_B)
    for j in range(D_B // LANES):
        qb_ref[:, j * LANES:(j + 1) * LANES] = rope(qb[:, j * LANES:(j + 1) * LANES]).astype(BF16)

    kvb = proj(3 * D_A + D_B, D_IN)
    kb = rope(kvb[:, :D_KV_B])
    vb = kvb[:, D_KV_B:]
    kb_ref[...] = kb.astype(BF16)
    vb_ref[...] = vb.astype(BF16)

    @pl.when(last)
    def _():
        kbt_ref[...] = kb[tm - keep_b:, :]
        vbt_ref[...] = vb[tm - keep_b:, :]


def _proj_call(x3, g_mix, w_p, cos_t, sa_t, sb_t, *, tm, keep_a, keep_b):
    nb, s, d = x3.shape
    assert s % tm == 0 and keep_a <= tm and keep_b <= tm
    grid = (nb, s // tm)

    def tok(width):
        return jax.ShapeDtypeStruct((nb, s, width), BF16)

    out_shape = ([tok(D_A)] * 4 + [tok(D_KV_B)] * 2
                 + [jax.ShapeDtypeStruct((nb, keep_a, D_A), F32)] * 2
                 + [jax.ShapeDtypeStruct((nb, keep_b, D_KV_B), F32)] * 2)
    tok_spec = lambda width: pl.BlockSpec((None, tm, width), lambda b, i: (b, i, 0))
    tail_spec = lambda rows, width: pl.BlockSpec((None, rows, width), lambda b, i: (b, 0, 0))
    tab_spec = pl.BlockSpec((tm, LANES), lambda b, i: (i, 0))
    in_specs = [
        tok_spec(d),
        pl.BlockSpec((1, d), lambda b, i: (0, 0)),
        pl.BlockSpec((d, D_IN), lambda b, i: (0, 0), pipeline_mode=pl.Buffered(1)),
        tab_spec, tab_spec, tab_spec,
    ]
    out_specs = ([tok_spec(D_A)] * 4 + [tok_spec(D_KV_B)] * 2
                 + [tail_spec(keep_a, D_A)] * 2 + [tail_spec(keep_b, D_KV_B)] * 2)
    return pl.pallas_call(
        functools.partial(_proj_body, tm=tm, keep_a=keep_a, keep_b=keep_b),
        grid=grid, in_specs=in_specs, out_specs=out_specs, out_shape=out_shape,
        compiler_params=_cparams(("arbitrary", "arbitrary")),
        name="proj",
    )(x3, g_mix, w_p, cos_t, sa_t, sb_t)


def _attend_a_quad(q, kw, vw, bias, hm_ref, thr):
    r = q.shape[0]
    qs = jnp.concatenate([q * hm_ref[h] for h in range(4)], axis=0)
    s = lax.dot_general(qs, kw, (((1,), (1,)), ((), ())), preferred_element_type=F32) + bias
    if thr is not None:
        col = lax.broadcasted_iota(jnp.int32, s.shape, 1)
        s = jnp.where(col >= thr, s, NEG_INF)
    m = jnp.max(s, axis=-1, keepdims=True)
    p = jnp.exp(s - m)
    denom = jnp.sum(p, axis=-1, keepdims=True)
    pv = jnp.dot(p.astype(BF16), vw, preferred_element_type=F32) * (1.0 / denom)
    lane_head = lax.broadcasted_iota(jnp.int32, (r, QUAD), 1) // HEAD_DIM
    o = pv[0:r]
    for h in range(1, 4):
        o = jnp.where(lane_head == h, pv[h * r:(h + 1) * r], o)
    return o


def _attend_b(q, kw, vw, maskrow, sink, lm_ref, thr):
    r = q.shape[0]
    pieces = []
    for p in range(N_PAIRS):
        qp = q[:, p * LANES:(p + 1) * LANES]
        pieces.append(qp * lm_ref[0])
        pieces.append(qp * lm_ref[1])
    qs = jnp.concatenate(pieces, axis=0)
    s = lax.dot_general(qs, kw, (((1,), (1,)), ((), ())), preferred_element_type=F32) + maskrow
    if thr is not None:
        col = lax.broadcasted_iota(jnp.int32, s.shape, 1)
        s = jnp.where(col >= thr, s, NEG_INF)
    m = jnp.maximum(jnp.max(s, axis=-1, keepdims=True), sink)
    p = jnp.exp(s - m)
    denom = jnp.sum(p, axis=-1, keepdims=True) + jnp.exp(sink - m)
    pv = jnp.dot(p.astype(BF16), vw, preferred_element_type=F32) * (1.0 / denom)
    lower = lax.broadcasted_iota(jnp.int32, (r, LANES), 1) < HEAD_DIM
    outs = [jnp.where(lower, pv[(2 * p) * r:(2 * p + 1) * r], pv[(2 * p + 1) * r:(2 * p + 2) * r])
            for p in range(N_PAIRS)]
    return jnp.concatenate(outs, axis=1)


def _group_norm_store(o_parts, g_ref, o_ref, rows, col0):
    width = sum(o.shape[1] for o in o_parts)
    ssq = sum(jnp.sum(o * o, axis=-1, keepdims=True) for o in o_parts)
    inv = lax.rsqrt(ssq * (1.0 / width) + EPS)
    off = 0
    for o in o_parts:
        w = o.shape[1]
        o_ref[rows, col0 + off:col0 + off + w] = ((o * inv) * g_ref[:, off:off + w]).astype(BF16)
        off += w


def _attn_prompt_body(qa_ref, kap_ref, kac_ref, vap_ref, vac_ref,
                      qb_ref, kbp_ref, kbc_ref, vbp_ref, vbc_ref,
                      bias_ref, hm_ref, lm_ref, maskb_ref, sink_ref, ga_ref, gb_ref,
                      o_ref, kwin, vwin, kbwin, vbwin):
    i = pl.program_id(1)
    for win, prev, cur in ((kwin, kap_ref, kac_ref), (vwin, vap_ref, vac_ref),
                           (kbwin, kbp_ref, kbc_ref), (vbwin, vbp_ref, vbc_ref)):
        win[0:TQ] = prev[...]
        win[TQ:2 * TQ] = cur[...]
        win[2 * TQ:2 * TQ + CHUNK] = jnp.zeros((CHUNK, win.shape[1]), BF16)

    def run(first_step):
        def body(c, carry):
            r0 = pl.multiple_of(c * CHUNK, CHUNK)
            rows = pl.ds(r0, CHUNK)
            thr_a = (A_PREV_CHUNKS - c) * CHUNK if first_step else None
            thr_b = jnp.maximum(B_PREV_CHUNKS - c, 0) * CHUNK if first_step else None
            parts = []
            for quad in range(N_QUADS):
                cols = slice(quad * QUAD, (quad + 1) * QUAD)
                parts.append(_attend_a_quad(
                    qa_ref[rows, cols], kwin[pl.ds(r0, A_WIN_PAD), cols], vwin[pl.ds(r0, A_WIN_PAD), cols],
                    bias_ref[quad], hm_ref, thr_a))
            _group_norm_store(parts, ga_ref, o_ref, rows, 0)
            b0 = pl.multiple_of(c * CHUNK + (TQ - B_WINDOW), CHUNK)
            ob = _attend_b(qb_ref[rows, :], kbwin[pl.ds(b0, B_WIN_PAD), :], vbwin[pl.ds(b0, B_WIN_PAD), :],
                           maskb_ref[...], sink_ref[...], lm_ref, thr_b)
            _group_norm_store([ob], gb_ref, o_ref, rows, D_A)
            return carry
        lax.fori_loop(0, TQ // CHUNK, body, 0)

    @pl.when(i == 0)
    def _():
        run(True)

    @pl.when(i > 0)
    def _():
        run(False)


def _attn_prompt_call(qa, ka, va, qb, kb, vb, bias, hm, lm, maskb, sink_rows, g_a, g_b):
    nb, s, _ = qa.shape
    assert s % TQ == 0
    grid = (nb, s // TQ)
    cur = lambda width: pl.BlockSpec((None, TQ, width), lambda b, i: (b, i, 0))
    prev = lambda width: pl.BlockSpec((None, TQ, width), lambda b, i: (b, jnp.maximum(i - 1, 0), 0))
    const = lambda shape: pl.BlockSpec(shape, lambda b, i: (0,) * len(shape))
    in_specs = [
        cur(D_A), prev(D_A), cur(D_A), prev(D_A), cur(D_A),
        cur(D_B), prev(D_KV_B), cur(D_KV_B), prev(D_KV_B), cur(D_KV_B),
        const(bias.shape), const(hm.shape), const(lm.shape), const(maskb.shape), const(sink_rows.shape),
        const(g_a.shape), const(g_b.shape),
    ]
    win_rows = 2 * TQ + CHUNK
    return pl.pallas_call(
        _attn_prompt_body,
        grid=grid, in_specs=in_specs,
        out_specs=pl.BlockSpec((None, TQ, D_A + D_B), lambda b, i: (b, i, 0)),
        out_shape=jax.ShapeDtypeStruct((nb, s, D_A + D_B), BF16),
        scratch_shapes=[pltpu.VMEM((win_rows, D_A), BF16), pltpu.VMEM((win_rows, D_A), BF16),
                        pltpu.VMEM((win_rows, D_KV_B), BF16), pltpu.VMEM((win_rows, D_KV_B), BF16)],
        compiler_params=_cparams(("arbitrary", "arbitrary")),
        name="attn_prompt",
    )(qa, ka, ka, va, va, qb, kb, kb, vb, vb, bias, hm, lm, maskb, sink_rows, g_a, g_b)


def _attn_sample_body(qa_ref, cka_ref, cva_ref, nka_ref, nva_ref,
                      qb_ref, ckb_ref, cvb_ref, nkb_ref, nvb_ref,
                      bias_ref, hm_ref, lm_ref, maskb_ref, sink_ref, ga_ref, gb_ref, o_ref, *, pad_a, pad_b):
    rows = slice(None)
    parts = []
    for quad in range(N_QUADS):
        cols = slice(quad * QUAD, (quad + 1) * QUAD)
        kw = jnp.concatenate([cka_ref[:, cols].astype(BF16), nka_ref[:, cols],
                              jnp.zeros((pad_a, QUAD), BF16)], axis=0)
        vw = jnp.concatenate([cva_ref[:, cols].astype(BF16), nva_ref[:, cols],
                              jnp.zeros((pad_a, QUAD), BF16)], axis=0)
        parts.append(_attend_a_quad(qa_ref[:, cols], kw, vw, bias_ref[quad], hm_ref, None))
    _group_norm_store(parts, ga_ref, o_ref, rows, 0)
    kbw = jnp.concatenate([ckb_ref[...].astype(BF16), nkb_ref[...], jnp.zeros((pad_b, D_KV_B), BF16)], axis=0)
    vbw = jnp.concatenate([cvb_ref[...].astype(BF16), nvb_ref[...], jnp.zeros((pad_b, D_KV_B), BF16)], axis=0)
    ob = _attend_b(qb_ref[...], kbw, vbw, maskb_ref[...], sink_ref[...], lm_ref, None)
    _group_norm_store([ob], gb_ref, o_ref, rows, D_A)


def _attn_sample_call(qa, cka, cva, nka, nva, qb, ckb, cvb, nkb, nvb, bias, hm, lm, maskb, sink_rows, g_a, g_b,
                      *, pad_a, pad_b):
    nb, s, _ = qa.shape
    blk = lambda a: pl.BlockSpec((None,) + a.shape[1:], lambda b: (b, 0, 0))
    const = lambda a: pl.BlockSpec(a.shape, lambda b: (0,) * a.ndim)
    per_batch = (qa, cka, cva, nka, nva, qb, ckb, cvb, nkb, nvb)
    consts = (bias, hm, lm, maskb, sink_rows, g_a, g_b)
    return pl.pallas_call(
        functools.partial(_attn_sample_body, pad_a=pad_a, pad_b=pad_b),
        grid=(nb,),
        in_specs=[blk(a) for a in per_batch] + [const(a) for a in consts],
        out_specs=pl.BlockSpec((None, s, D_A + D_B), lambda b: (b, 0, 0)),
        out_shape=jax.ShapeDtypeStruct((nb, s, D_A + D_B), BF16),
        compiler_params=_cparams(("arbitrary",)),
        name="attn_sample",
    )(*per_batch, *consts)


def _ffn_body(x_ref, o_ref, wo_ref, gf_ref, wg_ref, wu_ref, wd_ref, gfin_ref, y_ref, h_scr):
    k = pl.program_id(1)

    @pl.when(k == 0)
    def _():
        x1 = x_ref[...] + jnp.dot(o_ref[...], wo_ref[...], preferred_element_type=F32)
        y_ref[...] = x1
        ms = jnp.mean(x1 * x1, axis=-1, keepdims=True)
        h_scr[...] = ((x1 * lax.rsqrt(ms + EPS)) * gf_ref[...]).astype(BF16)

    h = h_scr[...]
    g = jnp.dot(h, wg_ref[...], preferred_element_type=F32)
    u = jnp.dot(h, wu_ref[...], preferred_element_type=F32)
    act = (g * jax.nn.sigmoid(g)) * u
    y_ref[...] += jnp.dot(act.astype(BF16), wd_ref[...], preferred_element_type=F32)

    @pl.when(k == pl.num_programs(1) - 1)
    def _():
        y = y_ref[...]
        ms = jnp.mean(y * y, axis=-1, keepdims=True)
        y_ref[...] = (y * lax.rsqrt(ms + EPS)) * gfin_ref[...]


def _ffn_call(x2, o2, w_out, g_ffn, w_g, w_u, w_d, g_fin, *, tm):
    n, d = x2.shape
    assert n % tm == 0
    grid = (n // tm, D_FF // TF)
    in_specs = [
        pl.BlockSpec((tm, d), lambda i, k: (i, 0)),
        pl.BlockSpec((tm, D_A + D_B), lambda i, k: (i, 0)),
        pl.BlockSpec((D_A + D_B, d), lambda i, k: (0, 0), pipeline_mode=pl.Buffered(1)),
        pl.BlockSpec((1, d), lambda i, k: (0, 0)),
        pl.BlockSpec((d, TF), lambda i, k: (0, k)),
        pl.BlockSpec((d, TF), lambda i, k: (0, k)),
        pl.BlockSpec((TF, d), lambda i, k: (k, 0)),
        pl.BlockSpec((1, d), lambda i, k: (0, 0)),
    ]
    return pl.pallas_call(
        _ffn_body,
        grid=grid, in_specs=in_specs,
        out_specs=pl.BlockSpec((tm, d), lambda i, k: (i, 0)),
        out_shape=jax.ShapeDtypeStruct((n, d), F32),
        scratch_shapes=[pltpu.VMEM((tm, d), BF16)],
        compiler_params=_cparams(("arbitrary", "arbitrary")),
        name="ffn",
    )(x2, o2, w_out, g_ffn, w_g, w_u, w_d, g_fin)


def _rope_tables(pos):
    half = ROPE_DIM // 2
    n = pos.shape[0]
    inv_freq = ROPE_THETA ** (-jnp.arange(half, dtype=F32) * 2.0 / ROPE_DIM)
    ang = pos.astype(F32)[:, None] * inv_freq[None, :]
    cos = jnp.cos(ang)
    sin = jnp.sin(ang)
    rest = HEAD_DIM - ROPE_DIM
    cos_h = jnp.concatenate([cos, cos, jnp.ones((n, rest), F32)], axis=1)
    sa_h = jnp.concatenate([-sin, jnp.zeros((n, HEAD_DIM - half), F32)], axis=1)
    sb_h = jnp.concatenate([jnp.zeros((n, half), F32), sin, jnp.zeros((n, rest), F32)], axis=1)
    rep = LANES // HEAD_DIM
    return tuple(jnp.tile(t, (1, rep)) for t in (cos_h, sa_h, sb_h))


def _bias_table(rel_table, nq, n_past, nk, width):
    u_max = nq - 1 + n_past
    idx = np.clip(u_max - np.arange(nq + nk - 1), -REL_CLIP, REL_CLIP) + REL_CLIP
    e = rel_table.astype(F32)[:, idx]
    b = jnp.stack([e[:, nq - 1 - q:nq - 1 - q + nk] for q in range(nq)], axis=1)
    b = jnp.pad(b, ((0, 0), (0, 0), (0, width - nk)), constant_values=NEG_INF)
    return b.reshape(N_QUADS, 4 * nq, width)


def _pad_mask_row(live, width):
    return jnp.concatenate([jnp.zeros((1, live), F32), jnp.full((1, width - live), NEG_INF, F32)], axis=1)


def _sink_rows(sinks, r):
    inter = sinks.astype(F32).reshape(N_KV_B, GQA_R).T.reshape(N_HEADS_B)
    return jnp.repeat(inter, r)[:, None]


def kernel(x_prompt, x_sample, cache_a_k, cache_a_v, cache_b_k, cache_b_v, w_in, norm_mix, rel_table, sinks,
           norm_grp_a, norm_grp_b, w_out, norm_ffn, w_gate, w_up, w_down, norm_final):
    assert w_in.shape[0] == 1, "single-layer problem"
    nb, s, d = x_prompt.shape
    sb, ss, _ = x_sample.shape
    keep_a = min(A_REACH, s)
    keep_b = min(B_WINDOW, s)
    scale = HEAD_DIM ** -0.5

    w = w_in[0]
    w_qb = w[:, 3 * D_A:3 * D_A + D_B].reshape(d, N_KV_B, GQA_R, HEAD_DIM).transpose(0, 2, 1, 3).reshape(d, D_B)
    w_p = jnp.concatenate([w[:, :D_A] * scale, w[:, D_A:3 * D_A], w_qb * scale, w[:, 3 * D_A + D_B:]],
                          axis=1).astype(BF16)
    g_mix = norm_mix[0][None, :]
    g_a = norm_grp_a[0][None, :]
    g_b = norm_grp_b[0].reshape(N_KV_B, GQA_R, HEAD_DIM).transpose(1, 0, 2).reshape(1, D_B)
    wo = w_out[0]
    wo_b = wo[D_A:].reshape(N_KV_B, GQA_R, HEAD_DIM, d).transpose(1, 0, 2, 3).reshape(D_B, d)
    wo_p = jnp.concatenate([wo[:D_A], wo_b], axis=0).astype(BF16)
    g_ffn = norm_ffn[0][None, :]
    w_g = w_gate[0].astype(BF16)
    w_u = w_up[0].astype(BF16)
    w_d = w_down[0].astype(BF16)
    g_fin = norm_final[None, :]

    def head_masks(width, rows):
        lane = jnp.arange(width) // HEAD_DIM
        m = (lane[None, :] == jnp.arange(width // HEAD_DIM)[:, None]).astype(BF16)
        return jnp.broadcast_to(m[:, None, :], (width // HEAD_DIM, rows, width))

    hm, lm = head_masks(QUAD, CHUNK), head_masks(LANES, CHUNK)
    hm_s, lm_s = head_masks(QUAD, ss), head_masks(LANES, ss)

    tables = _rope_tables(jnp.arange(s, dtype=jnp.int32))
    qa, ka, va, qb, kb, vb, ka_t, va_t, kb_t, vb_t = _proj_call(
        x_prompt, g_mix, w_p, *tables, tm=TQ, keep_a=keep_a, keep_b=keep_b)
    bias_p = _bias_table(rel_table[0], CHUNK, A_REACH, A_WIN, A_WIN_PAD)
    o_p = _attn_prompt_call(qa, ka, va, qb, kb, vb, bias_p, hm, lm, _pad_mask_row(B_WIN, B_WIN_PAD),
                            _sink_rows(sinks[0], CHUNK), g_a, g_b)
    y_p = _ffn_call(x_prompt.reshape(nb * s, d), o_p.reshape(nb * s, D_A + D_B), wo_p, g_ffn, w_g, w_u, w_d, g_fin,
                    tm=TQ).reshape(nb, s, d)

    ns = sb * ss
    pos_s = jnp.tile(PAST_LEN + jnp.arange(ss, dtype=jnp.int32), sb)
    tables_s = _rope_tables(pos_s)
    sqa, ska, sva, sqb, skb, svb, ska_t, sva_t, skb_t, svb_t = _proj_call(
        x_sample.reshape(1, ns, d), g_mix, w_p, *tables_s, tm=ns, keep_a=ns, keep_b=ns)
    ca_len = cache_a_k.shape[2]
    cb_len = cache_b_k.shape[2]
    live_a = ca_len + ss
    live_b = cb_len + ss
    wa = -(-live_a // LANES) * LANES
    wb = -(-live_b // LANES) * LANES
    bias_s = _bias_table(rel_table[0], ss, ca_len, live_a, wa)
    per_b = lambda a, width: a.reshape(sb, -1, width)
    o_s = _attn_sample_call(
        per_b(sqa, D_A), per_b(cache_a_k[0], D_A), per_b(cache_a_v[0], D_A), per_b(ska, D_A), per_b(sva, D_A),
        per_b(sqb, D_B), per_b(cache_b_k[0], D_KV_B), per_b(cache_b_v[0], D_KV_B), per_b(skb, D_KV_B),
        per_b(svb, D_KV_B),
        bias_s, hm_s, lm_s, _pad_mask_row(live_b, wb), _sink_rows(sinks[0], ss), g_a, g_b,
        pad_a=wa - live_a, pad_b=wb - live_b)
    y_s = _ffn_call(x_sample.reshape(ns, d), o_s.reshape(ns, D_A + D_B), wo_p, g_ffn, w_g, w_u, w_d, g_fin,
                    tm=ns).reshape(sb, ss, d)

    heads_a = lambda t, nbat: t.reshape(1, nbat, -1, N_HEADS_A, HEAD_DIM)
    heads_b = lambda t, nbat: t.reshape(1, nbat, -1, N_KV_B, HEAD_DIM)
    return (y_p, y_s,
            heads_a(ka_t, nb), heads_a(va_t, nb), heads_b(kb_t, nb), heads_b(vb_t, nb),
            heads_a(ska_t, sb), heads_a(sva_t, sb), heads_b(skb_t, sb), heads_b(svb_t, sb))
```

```python
import functools

import numpy as np
import jax
import jax.numpy as jnp
from jax import lax
from jax.experimental import pallas as pl
from jax.experimental.pallas import tpu as pltpu

D_MODEL = 2048
CHUNK = 64
HEAD_DIM = 64
N_HEADS_A = 16
N_HEADS_B = 16
N_KV_B = 2
GQA_R = N_HEADS_B // N_KV_B
D_A = N_HEADS_A * HEAD_DIM
D_B = N_HEADS_B * HEAD_DIM
D_KV_B = N_KV_B * HEAD_DIM
D_IN = 3 * D_A + D_B + 2 * D_KV_B
A_PREV_CHUNKS = 8
A_REACH = A_PREV_CHUNKS * CHUNK
REL_CLIP = 128
B_WINDOW = 128
B_PREV_CHUNKS = B_WINDOW // CHUNK
ROPE_THETA = 500000.0
ROPE_DIM = HEAD_DIM // 4
D_FF = 5632
NEG_INF = -1e30
EPS = 1e-6
PAST_LEN = 2048

LANES = 128
QUAD = 4 * HEAD_DIM
N_QUADS = N_HEADS_A // 4
N_PAIRS = N_HEADS_B // 2
TQ = 512
A_WIN = (A_PREV_CHUNKS + 1) * CHUNK
A_WIN_PAD = 640
B_WIN = (B_PREV_CHUNKS + 1) * CHUNK
B_WIN_PAD = 256
TF = 512
VMEM_LIMIT = 56 * 1024 * 1024

F32 = jnp.float32
BF16 = jnp.bfloat16


def _cparams(sem):
    return pltpu.CompilerParams(dimension_semantics=sem, vmem_limit_bytes=VMEM_LIMIT)


def _proj_body(x_ref, g_ref, w_ref, cos_ref, sa_ref, sb_ref,
               qa_ref, ka_ref, va_ref, qb_ref, kb_ref, vb_ref,
               kat_ref, vat_ref, kbt_ref, vbt_ref, *, tm, keep_a, keep_b):
    x = x_ref[...]
    ms = jnp.mean(x * x, axis=-1, keepdims=True)
    xn = ((x * lax.rsqrt(ms + EPS)) * g_ref[...]).astype(BF16)
    last = pl.program_id(1) == pl.num_programs(1) - 1

    def proj(lo, hi):
        return jnp.dot(xn, w_ref[:, lo:hi], preferred_element_type=F32)

    cos = cos_ref[...]
    sa = sa_ref[...]
    sb = sb_ref[...]

    def rope(y):
        return y * cos + pltpu.roll(y, LANES - ROPE_DIM // 2, 1) * sa + pltpu.roll(y, ROPE_DIM // 2, 1) * sb

    qa_ref[...] = proj(0, D_A).astype(BF16)

    ka = proj(D_A, 2 * D_A)
    ka_ref[...] = ka.astype(BF16)

    @pl.when(last)
    def _():
        kat_ref[...] = ka[tm - keep_a:, :]

    va = proj(2 * D_A, 3 * D_A)
    va_ref[...] = va.astype(BF16)

    @pl.when(last)
    def _():
        vat_ref[...] = va[tm - keep_a:, :]

    qb = proj(3 * D_A, 3 * D_A + D_B)
    for j in range(D_B // LANES):
        qb_ref[:, j * LANES:(j + 1) * LANES] = rope(qb[:, j * LANES:(j + 1) * LANES]).astype(BF16)

    kvb = proj(3 * D_A + D_B, D_IN)
    kb = rope(kvb[:, :D_KV_B])
    vb = kvb[:, D_KV_B:]
    kb_ref[...] = kb.astype(BF16)
    vb_ref[...] = vb.astype(BF16)

    @pl.when(last)
    def _():
        kbt_ref[...] = kb[tm - keep_b:, :]
        vbt_ref[...] = vb[tm - keep_b:, :]


def _proj_call(x3, g_mix, w_p, cos_t, sa_t, sb_t, *, tm, keep_a, keep_b):
    nb, s, d = x3.shape
    assert s % tm == 0 and keep_a <= tm and keep_b <= tm
    grid = (nb, s // tm)

    def tok(width):
        return jax.ShapeDtypeStruct((nb, s, width), BF16)

    out_shape = ([tok(D_A)] * 4 + [tok(D_KV_B)] * 2
                 + [jax.ShapeDtypeStruct((nb, keep_a, D_A), F32)] * 2
                 + [jax.ShapeDtypeStruct((nb, keep_b, D_KV_B), F32)] * 2)
    tok_spec = lambda width: pl.BlockSpec((None, tm, width), lambda b, i: (b, i, 0))
    tail_spec = lambda rows, width: pl.BlockSpec((None, rows, width), lambda b, i: (b, 0, 0))
    tab_spec = pl.BlockSpec((tm, LANES), lambda b, i: (i, 0))
    in_specs = [
        tok_spec(d),
        pl.BlockSpec((1, d), lambda b, i: (0, 0)),
        pl.BlockSpec((d, D_IN), lambda b, i: (0, 0), pipeline_mode=pl.Buffered(1)),
        tab_spec, tab_spec, tab_spec,
    ]
    out_specs = ([tok_spec(D_A)] * 4 + [tok_spec(D_KV_B)] * 2
                 + [tail_spec(keep_a, D_A)] * 2 + [tail_spec(keep_b, D_KV_B)] * 2)
    return pl.pallas_call(
        functools.partial(_proj_body, tm=tm, keep_a=keep_a, keep_b=keep_b),
        grid=grid, in_specs=in_specs, out_specs=out_specs, out_shape=out_shape,
        compiler_params=_cparams(("arbitrary", "arbitrary")),
        name="proj",
    )(x3, g_mix, w_p, cos_t, sa_t, sb_t)


def _attend_a_quad(q, kw, vw, bias, hm_ref, thr):
    r = q.shape[0]
    qs = jnp.concatenate([q * hm_ref[h] for h in range(4)], axis=0)
    s = lax.dot_general(qs, kw, (((1,), (1,)), ((), ())), preferred_element_type=F32) + bias
    if thr is not None:
        col = lax.broadcasted_iota(jnp.int32, s.shape, 1)
        s = jnp.where(col >= thr, s, NEG_INF)
    m = jnp.max(s, axis=-1, keepdims=True)
    p = jnp.exp(s - m)
    denom = jnp.sum(p, axis=-1, keepdims=True)
    pv = jnp.dot(p.astype(BF16), vw, preferred_element_type=F32) * (1.0 / denom)
    lane_head = lax.broadcasted_iota(jnp.int32, (r, QUAD), 1) // HEAD_DIM
    o = pv[0:r]
    for h in range(1, 4):
        o = jnp.where(lane_head == h, pv[h * r:(h + 1) * r], o)
    return o


def _attend_b(q, kw, vw, sbias, live_ref, lm_ref, thr):
    r = q.shape[0]
    pieces = []
    for p in range(N_PAIRS):
        qp = q[:, p * LANES:(p + 1) * LANES]
        pieces.append(qp * lm_ref[0])
        pieces.append(qp * lm_ref[1])
    qs = jnp.concatenate(pieces, axis=0)
    kz = kw * live_ref[0]
    vaug = jnp.concatenate([vw * live_ref[0], live_ref[1]], axis=1)
    s = lax.dot_general(qs, kz, (((1,), (1,)), ((), ())), preferred_element_type=F32) + sbias
    if thr is not None:
        col = lax.broadcasted_iota(jnp.int32, s.shape, 1)
        s = jnp.where(col >= thr, s, NEG_INF)
    m = jnp.max(s, axis=-1, keepdims=True)
    p = jnp.exp(s - m)
    pvd = jnp.dot(p.astype(BF16), vaug, preferred_element_type=F32)
    pv = pvd[:, :LANES] * (1.0 / pvd[:, LANES:])
    lower = lax.broadcasted_iota(jnp.int32, (r, LANES), 1) < HEAD_DIM
    outs = [jnp.where(lower, pv[(2 * p) * r:(2 * p + 1) * r], pv[(2 * p + 1) * r:(2 * p + 2) * r])
            for p in range(N_PAIRS)]
    return jnp.concatenate(outs, axis=1)


def _group_norm_store(o_parts, g_ref, o_ref, rows, col0):
    width = sum(o.shape[1] for o in o_parts)
    ssq = sum(jnp.sum(o * o, axis=-1, keepdims=True) for o in o_parts)
    inv = lax.rsqrt(ssq * (1.0 / width) + EPS)
    off = 0
    for o in o_parts:
        w = o.shape[1]
        o_ref[rows, col0 + off:col0 + off + w] = ((o * inv) * g_ref[:, off:off + w]).astype(BF16)
        off += w


def _attn_prompt_body(qa_ref, kap_ref, kac_ref, vap_ref, vac_ref,
                      qb_ref, kbp_ref, kbc_ref, vbp_ref, vbc_ref,
                      bias_ref, hm_ref, lm_ref, sbias_ref, live_ref, ga_ref, gb_ref,
                      o_ref, kwin, vwin, kbwin, vbwin):
    i = pl.program_id(1)
    for win, prev, cur in ((kwin, kap_ref, kac_ref), (vwin, vap_ref, vac_ref),
                           (kbwin, kbp_ref, kbc_ref), (vbwin, vbp_ref, vbc_ref)):
        win[0:TQ] = prev[...]
        win[TQ:2 * TQ] = cur[...]
        win[2 * TQ:2 * TQ + CHUNK] = jnp.zeros((CHUNK, win.shape[1]), BF16)

    def run(first_step):
        def body(c, carry):
            r0 = pl.multiple_of(c * CHUNK, CHUNK)
            rows = pl.ds(r0, CHUNK)
            thr_a = (A_PREV_CHUNKS - c) * CHUNK if first_step else None
            thr_b = jnp.maximum(B_PREV_CHUNKS - c, 0) * CHUNK if first_step else None
            parts = []
            for quad in range(N_QUADS):
                cols = slice(quad * QUAD, (quad + 1) * QUAD)
                parts.append(_attend_a_quad(
                    qa_ref[rows, cols], kwin[pl.ds(r0, A_WIN_PAD), cols], vwin[pl.ds(r0, A_WIN_PAD), cols],
                    bias_ref[quad], hm_ref, thr_a))
            _group_norm_store(parts, ga_ref, o_ref, rows, 0)
            b0 = pl.multiple_of(c * CHUNK + (TQ - B_WINDOW), CHUNK)
            ob = _attend_b(qb_ref[rows, :], kbwin[pl.ds(b0, B_WIN_PAD), :], vbwin[pl.ds(b0, B_WIN_PAD), :],
                           sbias_ref[...], live_ref, lm_ref, thr_b)
            _group_norm_store([ob], gb_ref, o_ref, rows, D_A)
            return carry
        lax.fori_loop(0, TQ // CHUNK, body, 0)

    @pl.when(i == 0)
    def _():
        run(True)

    @pl.when(i > 0)
    def _():
        run(False)


def _attn_prompt_call(qa, ka, va, qb, kb, vb, bias, hm, lm, sbias, live, g_a, g_b):
    nb, s, _ = qa.shape
    assert s % TQ == 0
    grid = (nb, s // TQ)
    cur = lambda width: pl.BlockSpec((None, TQ, width), lambda b, i: (b, i, 0))
    prev = lambda width: pl.BlockSpec((None, TQ, width), lambda b, i: (b, jnp.maximum(i - 1, 0), 0))
    const = lambda shape: pl.BlockSpec(shape, lambda b, i: (0,) * len(shape))
    in_specs = [
        cur(D_A), prev(D_A), cur(D_A), prev(D_A), cur(D_A),
        cur(D_B), prev(D_KV_B), cur(D_KV_B), prev(D_KV_B), cur(D_KV_B),
        const(bias.shape), const(hm.shape), const(lm.shape), const(sbias.shape), const(live.shape),
        const(g_a.shape), const(g_b.shape),
    ]
    win_rows = 2 * TQ + CHUNK
    return pl.pallas_call(
        _attn_prompt_body,
        grid=grid, in_specs=in_specs,
        out_specs=pl.BlockSpec((None, TQ, D_A + D_B), lambda b, i: (b, i, 0)),
        out_shape=jax.ShapeDtypeStruct((nb, s, D_A + D_B), BF16),
        scratch_shapes=[pltpu.VMEM((win_rows, D_A), BF16), pltpu.VMEM((win_rows, D_A), BF16),
                        pltpu.VMEM((win_rows, D_KV_B), BF16), pltpu.VMEM((win_rows, D_KV_B), BF16)],
        compiler_params=_cparams(("arbitrary", "arbitrary")),
        name="attn_prompt",
    )(qa, ka, ka, va, va, qb, kb, kb, vb, vb, bias, hm, lm, sbias, live, g_a, g_b)


def _attn_sample_body(qa_ref, cka_ref, cva_ref, nka_ref, nva_ref,
                      qb_ref, ckb_ref, cvb_ref, nkb_ref, nvb_ref,
                      bias_ref, hm_ref, lm_ref, sbias_ref, live_ref, ga_ref, gb_ref, o_ref, *, pad_a, pad_b):
    rows = slice(None)
    parts = []
    for quad in range(N_QUADS):
        cols = slice(quad * QUAD, (quad + 1) * QUAD)
        kw = jnp.concatenate([cka_ref[:, cols].astype(BF16), nka_ref[:, cols],
                              jnp.zeros((pad_a, QUAD), BF16)], axis=0)
        vw = jnp.concatenate([cva_ref[:, cols].astype(BF16), nva_ref[:, cols],
                              jnp.zeros((pad_a, QUAD), BF16)], axis=0)
        parts.append(_attend_a_quad(qa_ref[:, cols], kw, vw, bias_ref[quad], hm_ref, None))
    _group_norm_store(parts, ga_ref, o_ref, rows, 0)
    kbw = jnp.concatenate([ckb_ref[...].astype(BF16), nkb_ref[...], jnp.zeros((pad_b, D_KV_B), BF16)], axis=0)
    vbw = jnp.concatenate([cvb_ref[...].astype(BF16), nvb_ref[...], jnp.zeros((pad_b, D_KV_B), BF16)], axis=0)
    ob = _attend_b(qb_ref[...], kbw, vbw, sbias_ref[...], live_ref, lm_ref, None)
    _group_norm_store([ob], gb_ref, o_ref, rows, D_A)


def _attn_sample_call(qa, cka, cva, nka, nva, qb, ckb, cvb, nkb, nvb, bias, hm, lm, sbias, live, g_a, g_b,
                      *, pad_a, pad_b):
    nb, s, _ = qa.shape
    blk = lambda a: pl.BlockSpec((None,) + a.shape[1:], lambda b: (b, 0, 0))
    const = lambda a: pl.BlockSpec(a.shape, lambda b: (0,) * a.ndim)
    per_batch = (qa, cka, cva, nka, nva, qb, ckb, cvb, nkb, nvb)
    consts = (bias, hm, lm, sbias, live, g_a, g_b)
    return pl.pallas_call(
        functools.partial(_attn_sample_body, pad_a=pad_a, pad_b=pad_b),
        grid=(nb,),
        in_specs=[blk(a) for a in per_batch] + [const(a) for a in consts],
        out_specs=pl.BlockSpec((None, s, D_A + D_B), lambda b: (b, 0, 0)),
        out_shape=jax.ShapeDtypeStruct((nb, s, D_A + D_B), BF16),
        compiler_params=_cparams(("arbitrary",)),
        name="attn_sample",
    )(*per_batch, *consts)


def _ffn_body(x_ref, o_ref, wo_ref, gf_ref, wg_ref, wu_ref, wd_ref, gfin_ref, y_ref, h_scr):
    k = pl.program_id(1)

    @pl.when(k == 0)
    def _():
        x1 = x_ref[...] + jnp.dot(o_ref[...], wo_ref[...], preferred_element_type=F32)
        y_ref[...] = x1
        ms = jnp.mean(x1 * x1, axis=-1, keepdims=True)
        h_scr[...] = ((x1 * lax.rsqrt(ms + EPS)) * gf_ref[...]).astype(BF16)

    h = h_scr[...]
    g = jnp.dot(h, wg_ref[...], preferred_element_type=F32)
    u = jnp.dot(h, wu_ref[...], preferred_element_type=F32)
    act = (g * jax.nn.sigmoid(g)) * u
    y_ref[...] += jnp.dot(act.astype(BF16), wd_ref[...], preferred_element_type=F32)

    @pl.when(k == pl.num_programs(1) - 1)
    def _():
        y = y_ref[...]
        ms = jnp.mean(y * y, axis=-1, keepdims=True)
        y_ref[...] = (y * lax.rsqrt(ms + EPS)) * gfin_ref[...]


def _ffn_call(x2, o2, w_out, g_ffn, w_g, w_u, w_d, g_fin, *, tm):
    n, d = x2.shape
    assert n % tm == 0
    grid = (n // tm, D_FF // TF)
    in_specs = [
        pl.BlockSpec((tm, d), lambda i, k: (i, 0)),
        pl.BlockSpec((tm, D_A + D_B), lambda i, k: (i, 0)),
        pl.BlockSpec((D_A + D_B, d), lambda i, k: (0, 0), pipeline_mode=pl.Buffered(1)),
        pl.BlockSpec((1, d), lambda i, k: (0, 0)),
        pl.BlockSpec((d, TF), lambda i, k: (0, k)),
        pl.BlockSpec((d, TF), lambda i, k: (0, k)),
        pl.BlockSpec((TF, d), lambda i, k: (k, 0)),
        pl.BlockSpec((1, d), lambda i, k: (0, 0)),
    ]
    return pl.pallas_call(
        _ffn_body,
        grid=grid, in_specs=in_specs,
        out_specs=pl.BlockSpec((tm, d), lambda i, k: (i, 0)),
        out_shape=jax.ShapeDtypeStruct((n, d), F32),
        scratch_shapes=[pltpu.VMEM((tm, d), BF16)],
        compiler_params=_cparams(("arbitrary", "arbitrary")),
        name="ffn",
    )(x2, o2, w_out, g_ffn, w_g, w_u, w_d, g_fin)


def _rope_tables(pos):
    half = ROPE_DIM // 2
    n = pos.shape[0]
    inv_freq = ROPE_THETA ** (-jnp.arange(half, dtype=F32) * 2.0 / ROPE_DIM)
    ang = pos.astype(F32)[:, None] * inv_freq[None, :]
    cos = jnp.cos(ang)
    sin = jnp.sin(ang)
    rest = HEAD_DIM - ROPE_DIM
    cos_h = jnp.concatenate([cos, cos, jnp.ones((n, rest), F32)], axis=1)
    sa_h = jnp.concatenate([-sin, jnp.zeros((n, HEAD_DIM - half), F32)], axis=1)
    sb_h = jnp.concatenate([jnp.zeros((n, half), F32), sin, jnp.zeros((n, rest), F32)], axis=1)
    rep = LANES // HEAD_DIM
    return tuple(jnp.tile(t, (1, rep)) for t in (cos_h, sa_h, sb_h))


def _bias_table(rel_table, nq, n_past, nk, width):
    u_max = nq - 1 + n_past
    idx = np.clip(u_max - np.arange(nq + nk - 1), -REL_CLIP, REL_CLIP) + REL_CLIP
    e = rel_table.astype(F32)[:, idx]
    b = jnp.stack([e[:, nq - 1 - q:nq - 1 - q + nk] for q in range(nq)], axis=1)
    b = jnp.pad(b, ((0, 0), (0, 0), (0, width - nk)), constant_values=NEG_INF)
    return b.reshape(N_QUADS, 4 * nq, width)


def _sink_bias(sinks, r, live, width):
    assert live < width
    inter = sinks.astype(F32).reshape(N_KV_B, GQA_R).T.reshape(N_HEADS_B)
    rows = jnp.repeat(inter, r)[:, None]
    col = jnp.arange(width)[None, :]
    return jnp.where(col < live, 0.0, jnp.where(col == live, rows, NEG_INF)).astype(F32)


def _live_rows(live, width):
    row = jnp.arange(width)[:, None]
    return jnp.stack([jnp.broadcast_to(row < live, (width, LANES)),
                      jnp.broadcast_to(row <= live, (width, LANES))]).astype(BF16)


def kernel(x_prompt, x_sample, cache_a_k, cache_a_v, cache_b_k, cache_b_v, w_in, norm_mix, rel_table, sinks,
           norm_grp_a, norm_grp_b, w_out, norm_ffn, w_gate, w_up, w_down, norm_final):
    assert w_in.shape[0] == 1, "single-layer problem"
    nb, s, d = x_prompt.shape
    sb, ss, _ = x_sample.shape
    keep_a = min(A_REACH, s)
    keep_b = min(B_WINDOW, s)
    scale = HEAD_DIM ** -0.5

    w = w_in[0]
    w_qb = w[:, 3 * D_A:3 * D_A + D_B].reshape(d, N_KV_B, GQA_R, HEAD_DIM).transpose(0, 2, 1, 3).reshape(d, D_B)
    w_p = jnp.concatenate([w[:, :D_A] * scale, w[:, D_A:3 * D_A], w_qb * scale, w[:, 3 * D_A + D_B:]],
                          axis=1).astype(BF16)
    g_mix = norm_mix[0][None, :]
    g_a = norm_grp_a[0][None, :]
    g_b = norm_grp_b[0].reshape(N_KV_B, GQA_R, HEAD_DIM).transpose(1, 0, 2).reshape(1, D_B)
    wo = w_out[0]
    wo_b = wo[D_A:].reshape(N_KV_B, GQA_R, HEAD_DIM, d).transpose(1, 0, 2, 3).reshape(D_B, d)
    wo_p = jnp.concatenate([wo[:D_A], wo_b], axis=0).astype(BF16)
    g_ffn = norm_ffn[0][None, :]
    w_g = w_gate[0].astype(BF16)
    w_u = w_up[0].astype(BF16)
    w_d = w_down[0].astype(BF16)
    g_fin = norm_final[None, :]

    def head_masks(width, rows):
        lane = jnp.arange(width) // HEAD_DIM
        m = (lane[None, :] == jnp.arange(width // HEAD_DIM)[:, None]).astype(BF16)
        return jnp.broadcast_to(m[:, None, :], (width // HEAD_DIM, rows, width))

    hm, lm = head_masks(QUAD, CHUNK), head_masks(LANES, CHUNK)
    hm_s, lm_s = head_masks(QUAD, ss), head_masks(LANES, ss)

    tables = _rope_tables(jnp.arange(s, dtype=jnp.int32))
    qa, ka, va, qb, kb, vb, ka_t, va_t, kb_t, vb_t = _proj_call(
        x_prompt, g_mix, w_p, *tables, tm=TQ, keep_a=keep_a, keep_b=keep_b)
    bias_p = _bias_table(rel_table[0], CHUNK, A_REACH, A_WIN, A_WIN_PAD)
    o_p = _attn_prompt_call(qa, ka, va, qb, kb, vb, bias_p, hm, lm,
                            _sink_bias(sinks[0], CHUNK, B_WIN, B_WIN_PAD), _live_rows(B_WIN, B_WIN_PAD), g_a, g_b)
    y_p = _ffn_call(x_prompt.reshape(nb * s, d), o_p.reshape(nb * s, D_A + D_B), wo_p, g_ffn, w_g, w_u, w_d, g_fin,
                    tm=TQ).reshape(nb, s, d)

    ns = sb * ss
    pos_s = jnp.tile(PAST_LEN + jnp.arange(ss, dtype=jnp.int32), sb)
    tables_s = _rope_tables(pos_s)
    sqa, ska, sva, sqb, skb, svb, ska_t, sva_t, skb_t, svb_t = _proj_call(
        x_sample.reshape(1, ns, d), g_mix, w_p, *tables_s, tm=ns, keep_a=ns, keep_b=ns)
    ca_len = cache_a_k.shape[2]
    cb_len = cache_b_k.shape[2]
    live_a = ca_len + ss
    live_b = cb_len + ss
    wa = -(-live_a // LANES) * LANES
    wb = -(-live_b // LANES) * LANES
    bias_s = _bias_table(rel_table[0], ss, ca_len, live_a, wa)
    per_b = lambda a, width: a.reshape(sb, -1, width)
    o_s = _attn_sample_call(
        per_b(sqa, D_A), per_b(cache_a_k[0], D_A), per_b(cache_a_v[0], D_A), per_b(ska, D_A), per_b(sva, D_A),
        per_b(sqb, D_B), per_b(cache_b_k[0], D_KV_B), per_b(cache_b_v[0], D_KV_B), per_b(skb, D_KV_B),
        per_b(svb, D_KV_B),
        bias_s, hm_s, lm_s, _sink_bias(sinks[0], ss, live_b, wb), _live_rows(live_b, wb), g_a, g_b,
        pad_a=wa - live_a, pad_b=wb - live_b)
    y_s = _ffn_call(x_sample.reshape(ns, d), o_s.reshape(ns, D_A + D_B), wo_p, g_ffn, w_g, w_u, w_d, g_fin,
                    tm=ns).reshape(sb, ss, d)

    heads_a = lambda t, nbat: t.reshape(1, nbat, -1, N_HEADS_A, HEAD_DIM)
    heads_b = lambda t, nbat: t.reshape(1, nbat, -1, N_KV_B, HEAD_DIM)
    return (y_p, y_s,
            heads_a(ka_t, nb), heads_a(va_t, nb), heads_b(kb_t, nb), heads_b(vb_t, nb),
            heads_a(ska_t, sb), heads_a(sva_t, sb), heads_b(skb_t, sb), heads_b(svb_t, sb))
```

```python
import functools

import numpy as np
import jax
import jax.numpy as jnp
from jax import lax
from jax.experimental import pallas as pl
from jax.experimental.pallas import tpu as pltpu

D_MODEL = 2048
CHUNK = 64
HEAD_DIM = 64
N_HEADS_A = 16
N_HEADS_B = 16
N_KV_B = 2
GQA_R = N_HEADS_B // N_KV_B
D_A = N_HEADS_A * HEAD_DIM
D_B = N_HEADS_B * HEAD_DIM
D_KV_B = N_KV_B * HEAD_DIM
D_IN = 3 * D_A + D_B + 2 * D_KV_B
A_PREV_CHUNKS = 8
A_REACH = A_PREV_CHUNKS * CHUNK
REL_CLIP = 128
B_WINDOW = 128
B_PREV_CHUNKS = B_WINDOW // CHUNK
ROPE_THETA = 500000.0
ROPE_DIM = HEAD_DIM // 4
D_FF = 5632
NEG_INF = -1e30
EPS = 1e-6
PAST_LEN = 2048

LANES = 128
QUAD = 4 * HEAD_DIM
N_QUADS = N_HEADS_A // 4
N_PAIRS = N_HEADS_B // 2
TQ = 512
A_WIN = (A_PREV_CHUNKS + 1) * CHUNK
A_WIN_PAD = 640
B_WIN = (B_PREV_CHUNKS + 1) * CHUNK
B_WIN_PAD = 256
TF = 512
VMEM_LIMIT = 56 * 1024 * 1024

F32 = jnp.float32
BF16 = jnp.bfloat16


def _cparams(sem):
    return pltpu.CompilerParams(dimension_semantics=sem, vmem_limit_bytes=VMEM_LIMIT)


def _proj_body(x_ref, g_ref, w_ref, cos_ref, sa_ref, sb_ref,
               qa_ref, ka_ref, va_ref, qb_ref, kb_ref, vb_ref,
               kat_ref, vat_ref, kbt_ref, vbt_ref, *, tm, keep_a, keep_b, row_split):
    hm = tm // row_split
    for r0 in range(0, tm, hm):
        rows = slice(r0, r0 + hm)
        x = x_ref[rows, :]
        ms = jnp.mean(x * x, axis=-1, keepdims=True)
        xn = ((x * lax.rsqrt(ms + EPS)) * g_ref[...]).astype(BF16)

        def proj(lo, hi):
            return jnp.dot(xn, w_ref[:, lo:hi], preferred_element_type=F32)

        cos = cos_ref[rows, :]
        sa = sa_ref[rows, :]
        sb = sb_ref[rows, :]

        def rope(y):
            return y * cos + pltpu.roll(y, LANES - ROPE_DIM // 2, 1) * sa + pltpu.roll(y, ROPE_DIM // 2, 1) * sb

        def tail(dst_ref, val, keep):
            lo, hi = max(r0, tm - keep), r0 + hm
            if lo < hi:
                dst_ref[lo - (tm - keep):hi - (tm - keep), :] = val[lo - r0:hi - r0, :]

        qa_ref[rows, :] = proj(0, D_A).astype(BF16)

        ka = proj(D_A, 2 * D_A)
        ka_ref[rows, :] = ka.astype(BF16)
        tail(kat_ref, ka, keep_a)

        va = proj(2 * D_A, 3 * D_A)
        va_ref[rows, :] = va.astype(BF16)
        tail(vat_ref, va, keep_a)

        qb = proj(3 * D_A, 3 * D_A + D_B)
        for j in range(D_B // LANES):
            qb_ref[rows, j * LANES:(j + 1) * LANES] = rope(qb[:, j * LANES:(j + 1) * LANES]).astype(BF16)

        kvb = proj(3 * D_A + D_B, D_IN)
        kb = rope(kvb[:, :D_KV_B])
        vb = kvb[:, D_KV_B:]
        kb_ref[rows, :] = kb.astype(BF16)
        vb_ref[rows, :] = vb.astype(BF16)
        tail(kbt_ref, kb, keep_b)
        tail(vbt_ref, vb, keep_b)


def _proj_call(x3, g_mix, w_p, cos_t, sa_t, sb_t, *, tm, keep_a, keep_b, row_split):
    nb, s, d = x3.shape
    assert s % tm == 0 and keep_a <= tm and keep_b <= tm and tm % row_split == 0
    grid = (nb, s // tm)

    def tok(width):
        return jax.ShapeDtypeStruct((nb, s, width), BF16)

    out_shape = ([tok(D_A)] * 4 + [tok(D_KV_B)] * 2
                 + [jax.ShapeDtypeStruct((nb, keep_a, D_A), F32)] * 2
                 + [jax.ShapeDtypeStruct((nb, keep_b, D_KV_B), F32)] * 2)
    tok_spec = lambda width: pl.BlockSpec((None, tm, width), lambda b, i: (b, i, 0))
    tail_spec = lambda rows, width: pl.BlockSpec((None, rows, width), lambda b, i: (b, 0, 0))
    tab_spec = pl.BlockSpec((tm, LANES), lambda b, i: (i, 0))
    in_specs = [
        tok_spec(d),
        pl.BlockSpec((1, d), lambda b, i: (0, 0)),
        pl.BlockSpec((d, D_IN), lambda b, i: (0, 0), pipeline_mode=pl.Buffered(1)),
        tab_spec, tab_spec, tab_spec,
    ]
    out_specs = ([tok_spec(D_A)] * 4 + [tok_spec(D_KV_B)] * 2
                 + [tail_spec(keep_a, D_A)] * 2 + [tail_spec(keep_b, D_KV_B)] * 2)
    return pl.pallas_call(
        functools.partial(_proj_body, tm=tm, keep_a=keep_a, keep_b=keep_b, row_split=row_split),
        grid=grid, in_specs=in_specs, out_specs=out_specs, out_shape=out_shape,
        compiler_params=_cparams(("arbitrary", "arbitrary")),
        name="proj",
    )(x3, g_mix, w_p, cos_t, sa_t, sb_t)


def _attend_a_quad(q, kw, vw, bias, hm_ref, thr):
    r = q.shape[0]
    qs = jnp.concatenate([q * hm_ref[h] for h in range(4)], axis=0)
    s = lax.dot_general(qs, kw, (((1,), (1,)), ((), ())), preferred_element_type=F32) + bias
    if thr is not None:
        col = lax.broadcasted_iota(jnp.int32, s.shape, 1)
        s = jnp.where(col >= thr, s, NEG_INF)
    m = jnp.max(s, axis=-1, keepdims=True)
    p = jnp.exp(s - m)
    denom = jnp.sum(p, axis=-1, keepdims=True)
    pv = jnp.dot(p.astype(BF16), vw, preferred_element_type=F32) * (1.0 / denom)
    lane_head = lax.broadcasted_iota(jnp.int32, (r, QUAD), 1) // HEAD_DIM
    o = pv[0:r]
    for h in range(1, 4):
        o = jnp.where(lane_head == h, pv[h * r:(h + 1) * r], o)
    return o


def _attend_b(q, kw, vw, sbias, live_ref, lm_ref, thr):
    r = q.shape[0]
    pieces = []
    for p in range(N_PAIRS):
        qp = q[:, p * LANES:(p + 1) * LANES]
        pieces.append(qp * lm_ref[0])
        pieces.append(qp * lm_ref[1])
    qs = jnp.concatenate(pieces, axis=0)
    kz = kw * live_ref[0]
    vaug = jnp.concatenate([vw * live_ref[0], live_ref[1]], axis=1)
    s = lax.dot_general(qs, kz, (((1,), (1,)), ((), ())), preferred_element_type=F32) + sbias
    if thr is not None:
        col = lax.broadcasted_iota(jnp.int32, s.shape, 1)
        s = jnp.where(col >= thr, s, NEG_INF)
    m = jnp.max(s, axis=-1, keepdims=True)
    p = jnp.exp(s - m)
    pvd = jnp.dot(p.astype(BF16), vaug, preferred_element_type=F32)
    pv = pvd[:, :LANES] * (1.0 / pvd[:, LANES:])
    lower = lax.broadcasted_iota(jnp.int32, (r, LANES), 1) < HEAD_DIM
    outs = [jnp.where(lower, pv[(2 * p) * r:(2 * p + 1) * r], pv[(2 * p + 1) * r:(2 * p + 2) * r])
            for p in range(N_PAIRS)]
    return jnp.concatenate(outs, axis=1)


def _group_norm_store(o_parts, g_ref, o_ref, rows, col0):
    width = sum(o.shape[1] for o in o_parts)
    ssq = sum(jnp.sum(o * o, axis=-1, keepdims=True) for o in o_parts)
    inv = lax.rsqrt(ssq * (1.0 / width) + EPS)
    off = 0
    for o in o_parts:
        w = o.shape[1]
        o_ref[rows, col0 + off:col0 + off + w] = ((o * inv) * g_ref[:, off:off + w]).astype(BF16)
        off += w


def _attn_prompt_body(qa_ref, kap_ref, kac_ref, vap_ref, vac_ref,
                      qb_ref, kbp_ref, kbc_ref, vbp_ref, vbc_ref,
                      bias_ref, hm_ref, lm_ref, sbias_ref, live_ref, ga_ref, gb_ref,
                      o_ref, kwin, vwin, kbwin, vbwin):
    i = pl.program_id(1)
    for win, prev, cur in ((kwin, kap_ref, kac_ref), (vwin, vap_ref, vac_ref),
                           (kbwin, kbp_ref, kbc_ref), (vbwin, vbp_ref, vbc_ref)):
        win[0:TQ] = prev[...]
        win[TQ:2 * TQ] = cur[...]
        win[2 * TQ:2 * TQ + CHUNK] = jnp.zeros((CHUNK, win.shape[1]), BF16)

    def run(first_step):
        def body(c, carry):
            r0 = pl.multiple_of(c * CHUNK, CHUNK)
            rows = pl.ds(r0, CHUNK)
            thr_a = (A_PREV_CHUNKS - c) * CHUNK if first_step else None
            thr_b = jnp.maximum(B_PREV_CHUNKS - c, 0) * CHUNK if first_step else None
            parts = []
            for quad in range(N_QUADS):
                cols = slice(quad * QUAD, (quad + 1) * QUAD)
                parts.append(_attend_a_quad(
                    qa_ref[rows, cols], kwin[pl.ds(r0, A_WIN_PAD), cols], vwin[pl.ds(r0, A_WIN_PAD), cols],
                    bias_ref[quad], hm_ref, thr_a))
            _group_norm_store(parts, ga_ref, o_ref, rows, 0)
            b0 = pl.multiple_of(c * CHUNK + (TQ - B_WINDOW), CHUNK)
            ob = _attend_b(qb_ref[rows, :], kbwin[pl.ds(b0, B_WIN_PAD), :], vbwin[pl.ds(b0, B_WIN_PAD), :],
                           sbias_ref[...], live_ref, lm_ref, thr_b)
            _group_norm_store([ob], gb_ref, o_ref, rows, D_A)
            return carry
        lax.fori_loop(0, TQ // CHUNK, body, 0, unroll=2)

    @pl.when(i == 0)
    def _():
        run(True)

    @pl.when(i > 0)
    def _():
        run(False)


def _attn_prompt_call(qa, ka, va, qb, kb, vb, bias, hm, lm, sbias, live, g_a, g_b):
    nb, s, _ = qa.shape
    assert s % TQ == 0
    grid = (nb, s // TQ)
    cur = lambda width: pl.BlockSpec((None, TQ, width), lambda b, i: (b, i, 0))
    prev = lambda width: pl.BlockSpec((None, TQ, width), lambda b, i: (b, jnp.maximum(i - 1, 0), 0))
    const = lambda shape: pl.BlockSpec(shape, lambda b, i: (0,) * len(shape))
    in_specs = [
        cur(D_A), prev(D_A), cur(D_A), prev(D_A), cur(D_A),
        cur(D_B), prev(D_KV_B), cur(D_KV_B), prev(D_KV_B), cur(D_KV_B),
        const(bias.shape), const(hm.shape), const(lm.shape), const(sbias.shape), const(live.shape),
        const(g_a.shape), const(g_b.shape),
    ]
    win_rows = 2 * TQ + CHUNK
    return pl.pallas_call(
        _attn_prompt_body,
        grid=grid, in_specs=in_specs,
        out_specs=pl.BlockSpec((None, TQ, D_A + D_B), lambda b, i: (b, i, 0)),
        out_shape=jax.ShapeDtypeStruct((nb, s, D_A + D_B), BF16),
        scratch_shapes=[pltpu.VMEM((win_rows, D_A), BF16), pltpu.VMEM((win_rows, D_A), BF16),
                        pltpu.VMEM((win_rows, D_KV_B), BF16), pltpu.VMEM((win_rows, D_KV_B), BF16)],
        compiler_params=_cparams(("arbitrary", "arbitrary")),
        name="attn_prompt",
    )(qa, ka, ka, va, va, qb, kb, kb, vb, vb, bias, hm, lm, sbias, live, g_a, g_b)


def _attn_sample_body(qa_ref, cka_ref, cva_ref, nka_ref, nva_ref,
                      qb_ref, ckb_ref, cvb_ref, nkb_ref, nvb_ref,
                      bias_ref, hm_ref, lm_ref, sbias_ref, live_ref, ga_ref, gb_ref, o_ref, *, pad_a, pad_b):
    rows = slice(None)
    parts = []
    for quad in range(N_QUADS):
        cols = slice(quad * QUAD, (quad + 1) * QUAD)
        kw = jnp.concatenate([cka_ref[:, cols].astype(BF16), nka_ref[:, cols],
                              jnp.zeros((pad_a, QUAD), BF16)], axis=0)
        vw = jnp.concatenate([cva_ref[:, cols].astype(BF16), nva_ref[:, cols],
                              jnp.zeros((pad_a, QUAD), BF16)], axis=0)
        parts.append(_attend_a_quad(qa_ref[:, cols], kw, vw, bias_ref[quad], hm_ref, None))
    _group_norm_store(parts, ga_ref, o_ref, rows, 0)
    kbw = jnp.concatenate([ckb_ref[...].astype(BF16), nkb_ref[...], jnp.zeros((pad_b, D_KV_B), BF16)], axis=0)
    vbw = jnp.concatenate([cvb_ref[...].astype(BF16), nvb_ref[...], jnp.zeros((pad_b, D_KV_B), BF16)], axis=0)
    ob = _attend_b(qb_ref[...], kbw, vbw, sbias_ref[...], live_ref, lm_ref, None)
    _group_norm_store([ob], gb_ref, o_ref, rows, D_A)


def _attn_sample_call(qa, cka, cva, nka, nva, qb, ckb, cvb, nkb, nvb, bias, hm, lm, sbias, live, g_a, g_b,
                      *, pad_a, pad_b):
    nb, s, _ = qa.shape
    blk = lambda a: pl.BlockSpec((None,) + a.shape[1:], lambda b: (b, 0, 0))
    const = lambda a: pl.BlockSpec(a.shape, lambda b: (0,) * a.ndim)
    per_batch = (qa, cka, cva, nka, nva, qb, ckb, cvb, nkb, nvb)
    consts = (bias, hm, lm, sbias, live, g_a, g_b)
    return pl.pallas_call(
        functools.partial(_attn_sample_body, pad_a=pad_a, pad_b=pad_b),
        grid=(nb,),
        in_specs=[blk(a) for a in per_batch] + [const(a) for a in consts],
        out_specs=pl.BlockSpec((None, s, D_A + D_B), lambda b: (b, 0, 0)),
        out_shape=jax.ShapeDtypeStruct((nb, s, D_A + D_B), BF16),
        compiler_params=_cparams(("arbitrary",)),
        name="attn_sample",
    )(*per_batch, *consts)


def _ffn_body(x_ref, o_ref, wo_ref, gf_ref, wg_ref, wu_ref, wd_ref, gfin_ref, y_ref, h_scr):
    k = pl.program_id(1)

    @pl.when(k == 0)
    def _():
        x1 = x_ref[...] + jnp.dot(o_ref[...], wo_ref[...], preferred_element_type=F32)
        y_ref[...] = x1
        ms = jnp.mean(x1 * x1, axis=-1, keepdims=True)
        h_scr[...] = ((x1 * lax.rsqrt(ms + EPS)) * gf_ref[...]).astype(BF16)

    h = h_scr[...]
    g = jnp.dot(h, wg_ref[...], preferred_element_type=F32)
    u = jnp.dot(h, wu_ref[...], preferred_element_type=F32)
    act = (g * jax.nn.sigmoid(g)) * u
    y_ref[...] += jnp.dot(act.astype(BF16), wd_ref[...], preferred_element_type=F32)

    @pl.when(k == pl.num_programs(1) - 1)
    def _():
        y = y_ref[...]
        ms = jnp.mean(y * y, axis=-1, keepdims=True)
        y_ref[...] = (y * lax.rsqrt(ms + EPS)) * gfin_ref[...]


def _ffn_call(x2, o2, w_out, g_ffn, w_g, w_u, w_d, g_fin, *, tm):
    n, d = x2.shape
    assert n % tm == 0
    grid = (n // tm, D_FF // TF)
    in_specs = [
        pl.BlockSpec((tm, d), lambda i, k: (i, 0)),
        pl.BlockSpec((tm, D_A + D_B), lambda i, k: (i, 0)),
        pl.BlockSpec((D_A + D_B, d), lambda i, k: (0, 0), pipeline_mode=pl.Buffered(1)),
        pl.BlockSpec((1, d), lambda i, k: (0, 0)),
        pl.BlockSpec((d, TF), lambda i, k: (0, k)),
        pl.BlockSpec((d, TF), lambda i, k: (0, k)),
        pl.BlockSpec((TF, d), lambda i, k: (k, 0)),
        pl.BlockSpec((1, d), lambda i, k: (0, 0)),
    ]
    return pl.pallas_call(
        _ffn_body,
        grid=grid, in_specs=in_specs,
        out_specs=pl.BlockSpec((tm, d), lambda i, k: (i, 0)),
        out_shape=jax.ShapeDtypeStruct((n, d), F32),
        scratch_shapes=[pltpu.VMEM((tm, d), BF16)],
        compiler_params=_cparams(("arbitrary", "arbitrary")),
        name="ffn",
    )(x2, o2, w_out, g_ffn, w_g, w_u, w_d, g_fin)


def _rope_tables(pos):
    half = ROPE_DIM // 2
    n = pos.shape[0]
    inv_freq = ROPE_THETA ** (-jnp.arange(half, dtype=F32) * 2.0 / ROPE_DIM)
    ang = pos.astype(F32)[:, None] * inv_freq[None, :]
    cos = jnp.cos(ang)
    sin = jnp.sin(ang)
    rest = HEAD_DIM - ROPE_DIM
    cos_h = jnp.concatenate([cos, cos, jnp.ones((n, rest), F32)], axis=1)
    sa_h = jnp.concatenate([-sin, jnp.zeros((n, HEAD_DIM - half), F32)], axis=1)
    sb_h = jnp.concatenate([jnp.zeros((n, half), F32), sin, jnp.zeros((n, rest), F32)], axis=1)
    rep = LANES // HEAD_DIM
    return tuple(jnp.tile(t, (1, rep)) for t in (cos_h, sa_h, sb_h))


def _bias_table(rel_table, nq, n_past, nk, width):
    u_max = nq - 1 + n_past
    idx = np.clip(u_max - np.arange(nq + nk - 1), -REL_CLIP, REL_CLIP) + REL_CLIP
    e = rel_table.astype(F32)[:, idx]
    b = jnp.stack([e[:, nq - 1 - q:nq - 1 - q + nk] for q in range(nq)], axis=1)
    b = jnp.pad(b, ((0, 0), (0, 0), (0, width - nk)), constant_values=NEG_INF)
    return b.reshape(N_QUADS, 4 * nq, width)


def _sink_bias(sinks, r, live, width):
    assert live < width
    inter = sinks.astype(F32).reshape(N_KV_B, GQA_R).T.reshape(N_HEADS_B)
    rows = jnp.repeat(inter, r)[:, None]
    col = jnp.arange(width)[None, :]
    return jnp.where(col < live, 0.0, jnp.where(col == live, rows, NEG_INF)).astype(F32)


def _live_rows(live, width):
    row = jnp.arange(width)[:, None]
    return jnp.stack([jnp.broadcast_to(row < live, (width, LANES)),
                      jnp.broadcast_to(row <= live, (width, LANES))]).astype(BF16)


def kernel(x_prompt, x_sample, cache_a_k, cache_a_v, cache_b_k, cache_b_v, w_in, norm_mix, rel_table, sinks,
           norm_grp_a, norm_grp_b, w_out, norm_ffn, w_gate, w_up, w_down, norm_final):
    assert w_in.shape[0] == 1, "single-layer problem"
    nb, s, d = x_prompt.shape
    sb, ss, _ = x_sample.shape
    keep_a = min(A_REACH, s)
    keep_b = min(B_WINDOW, s)
    scale = HEAD_DIM ** -0.5

    w = w_in[0]
    w_qb = w[:, 3 * D_A:3 * D_A + D_B].reshape(d, N_KV_B, GQA_R, HEAD_DIM).transpose(0, 2, 1, 3).reshape(d, D_B)
    w_p = jnp.concatenate([w[:, :D_A] * scale, w[:, D_A:3 * D_A], w_qb * scale, w[:, 3 * D_A + D_B:]],
                          axis=1).astype(BF16)
    g_mix = norm_mix[0][None, :]
    g_a = norm_grp_a[0][None, :]
    g_b = norm_grp_b[0].reshape(N_KV_B, GQA_R, HEAD_DIM).transpose(1, 0, 2).reshape(1, D_B)
    wo = w_out[0]
    wo_b = wo[D_A:].reshape(N_KV_B, GQA_R, HEAD_DIM, d).transpose(1, 0, 2, 3).reshape(D_B, d)
    wo_p = jnp.concatenate([wo[:D_A], wo_b], axis=0).astype(BF16)
    g_ffn = norm_ffn[0][None, :]
    w_g = w_gate[0].astype(BF16)
    w_u = w_up[0].astype(BF16)
    w_d = w_down[0].astype(BF16)
    g_fin = norm_final[None, :]

    def head_masks(width, rows):
        lane = jnp.arange(width) // HEAD_DIM
        m = (lane[None, :] == jnp.arange(width // HEAD_DIM)[:, None]).astype(BF16)
        return jnp.broadcast_to(m[:, None, :], (width // HEAD_DIM, rows, width))

    hm, lm = head_masks(QUAD, CHUNK), head_masks(LANES, CHUNK)
    hm_s, lm_s = head_masks(QUAD, ss), head_masks(LANES, ss)

    tables = _rope_tables(jnp.arange(s, dtype=jnp.int32))
    qa, ka, va, qb, kb, vb, ka_t, va_t, kb_t, vb_t = _proj_call(
        x_prompt, g_mix, w_p, *tables, tm=TQ, keep_a=keep_a, keep_b=keep_b, row_split=2)
    bias_p = _bias_table(rel_table[0], CHUNK, A_REACH, A_WIN, A_WIN_PAD)
    o_p = _attn_prompt_call(qa, ka, va, qb, kb, vb, bias_p, hm, lm,
                            _sink_bias(sinks[0], CHUNK, B_WIN, B_WIN_PAD), _live_rows(B_WIN, B_WIN_PAD), g_a, g_b)
    y_p = _ffn_call(x_prompt.reshape(nb * s, d), o_p.reshape(nb * s, D_A + D_B), wo_p, g_ffn, w_g, w_u, w_d, g_fin,
                    tm=TQ).reshape(nb, s, d)

    ns = sb * ss
    pos_s = jnp.tile(PAST_LEN + jnp.arange(ss, dtype=jnp.int32), sb)
    tables_s = _rope_tables(pos_s)
    sqa, ska, sva, sqb, skb, svb, ska_t, sva_t, skb_t, svb_t = _proj_call(
        x_sample.reshape(1, ns, d), g_mix, w_p, *tables_s, tm=ns, keep_a=ns, keep_b=ns, row_split=1)
    ca_len = cache_a_k.shape[2]
    cb_len = cache_b_k.shape[2]
    live_a = ca_len + ss
    live_b = cb_len + ss
    wa = -(-live_a // LANES) * LANES
    wb = -(-live_b // LANES) * LANES
    bias_s = _bias_table(rel_table[0], ss, ca_len, live_a, wa)
    per_b = lambda a, width: a.reshape(sb, -1, width)
    o_s = _attn_sample_call(
        per_b(sqa, D_A), per_b(cache_a_k[0], D_A), per_b(cache_a_v[0], D_A), per_b(ska, D_A), per_b(sva, D_A),
        per_b(sqb, D_B), per_b(cache_b_k[0], D_KV_B), per_b(cache_b_v[0], D_KV_B), per_b(skb, D_KV_B),
        per_b(svb, D_KV_B),
        bias_s, hm_s, lm_s, _sink_bias(sinks[0], ss, live_b, wb), _live_rows(live_b, wb), g_a, g_b,
        pad_a=wa - live_a, pad_b=wb - live_b)
    y_s = _ffn_call(x_sample.reshape(ns, d), o_s.reshape(ns, D_A + D_B), wo_p, g_ffn, w_g, w_u, w_d, g_fin,
                    tm=ns).reshape(sb, ss, d)

    heads_a = lambda t, nbat: t.reshape(1, nbat, -1, N_HEADS_A, HEAD_DIM)
    heads_b = lambda t, nbat: t.reshape(1, nbat, -1, N_KV_B, HEAD_DIM)
    return (y_p, y_s,
            heads_a(ka_t, nb), heads_a(va_t, nb), heads_b(kb_t, nb), heads_b(vb_t, nb),
            heads_a(ska_t, sb), heads_a(sva_t, sb), heads_b(skb_t, sb), heads_b(svb_t, sb))
```

```python
import functools

import numpy as np
import jax
import jax.numpy as jnp
from jax import lax
from jax.experimental import pallas as pl
from jax.experimental.pallas import tpu as pltpu

D_MODEL = 2048
CHUNK = 64
HEAD_DIM = 64
N_HEADS_A = 16
N_HEADS_B = 16
N_KV_B = 2
GQA_R = N_HEADS_B // N_KV_B
D_A = N_HEADS_A * HEAD_DIM
D_B = N_HEADS_B * HEAD_DIM
D_KV_B = N_KV_B * HEAD_DIM
D_IN = 3 * D_A + D_B + 2 * D_KV_B
A_PREV_CHUNKS = 8
A_REACH = A_PREV_CHUNKS * CHUNK
REL_CLIP = 128
B_WINDOW = 128
B_PREV_CHUNKS = B_WINDOW // CHUNK
ROPE_THETA = 500000.0
ROPE_DIM = HEAD_DIM // 4
D_FF = 5632
NEG_INF = -1e30
EPS = 1e-6
PAST_LEN = 2048

LANES = 128
QUAD = 4 * HEAD_DIM
N_QUADS = N_HEADS_A // 4
N_PAIRS = N_HEADS_B // 2
TQ = 512
A_WIN = (A_PREV_CHUNKS + 1) * CHUNK
A_WIN_PAD = 640
B_WIN = (B_PREV_CHUNKS + 1) * CHUNK
B_WIN_PAD = 256
TF = 512
VMEM_LIMIT = 56 * 1024 * 1024

F32 = jnp.float32
BF16 = jnp.bfloat16


def _cparams(sem):
    return pltpu.CompilerParams(dimension_semantics=sem, vmem_limit_bytes=VMEM_LIMIT)


def _proj_body(x_ref, g_ref, w_ref, cos_ref, sa_ref, sb_ref,
               qa_ref, ka_ref, va_ref, qb_ref, kb_ref, vb_ref,
               kat_ref, vat_ref, kbt_ref, vbt_ref, *, tm, keep_a, keep_b, row_split):
    hm = tm // row_split
    for r0 in range(0, tm, hm):
        rows = slice(r0, r0 + hm)
        x = x_ref[rows, :]
        ms = jnp.mean(x * x, axis=-1, keepdims=True)
        xn = ((x * lax.rsqrt(ms + EPS)) * g_ref[...]).astype(BF16)

        def proj(lo, hi):
            return jnp.dot(xn, w_ref[:, lo:hi], preferred_element_type=F32)

        cos = cos_ref[rows, :]
        sa = sa_ref[rows, :]
        sb = sb_ref[rows, :]

        def rope(y):
            return y * cos + pltpu.roll(y, LANES - ROPE_DIM // 2, 1) * sa + pltpu.roll(y, ROPE_DIM // 2, 1) * sb

        def tail(dst_ref, val, keep):
            lo, hi = max(r0, tm - keep), r0 + hm
            if lo < hi:
                dst_ref[lo - (tm - keep):hi - (tm - keep), :] = val[lo - r0:hi - r0, :]

        qa_ref[rows, :] = proj(0, D_A).astype(BF16)

        ka = proj(D_A, 2 * D_A)
        ka_ref[rows, :] = ka.astype(BF16)
        tail(kat_ref, ka, keep_a)

        va = proj(2 * D_A, 3 * D_A)
        va_ref[rows, :] = va.astype(BF16)
        tail(vat_ref, va, keep_a)

        qb = proj(3 * D_A, 3 * D_A + D_B)
        for j in range(D_B // LANES):
            qb_ref[rows, j * LANES:(j + 1) * LANES] = rope(qb[:, j * LANES:(j + 1) * LANES]).astype(BF16)

        kvb = proj(3 * D_A + D_B, D_IN)
        kb = rope(kvb[:, :D_KV_B])
        vb = kvb[:, D_KV_B:]
        kb_ref[rows, :] = kb.astype(BF16)
        vb_ref[rows, :] = vb.astype(BF16)
        tail(kbt_ref, kb, keep_b)
        tail(vbt_ref, vb, keep_b)


def _proj_call(x3, g_mix, w_p, cos_t, sa_t, sb_t, *, tm, keep_a, keep_b, row_split):
    nb, s, d = x3.shape
    assert s % tm == 0 and keep_a <= tm and keep_b <= tm and tm % row_split == 0
    grid = (nb, s // tm)

    def tok(width):
        return jax.ShapeDtypeStruct((nb, s, width), BF16)

    out_shape = ([tok(D_A)] * 4 + [tok(D_KV_B)] * 2
                 + [jax.ShapeDtypeStruct((nb, keep_a, D_A), F32)] * 2
                 + [jax.ShapeDtypeStruct((nb, keep_b, D_KV_B), F32)] * 2)
    tok_spec = lambda width: pl.BlockSpec((None, tm, width), lambda b, i: (b, i, 0))
    tail_spec = lambda rows, width: pl.BlockSpec((None, rows, width), lambda b, i: (b, 0, 0))
    tab_spec = pl.BlockSpec((tm, LANES), lambda b, i: (i, 0))
    in_specs = [
        tok_spec(d),
        pl.BlockSpec((1, d), lambda b, i: (0, 0)),
        pl.BlockSpec((d, D_IN), lambda b, i: (0, 0), pipeline_mode=pl.Buffered(1)),
        tab_spec, tab_spec, tab_spec,
    ]
    out_specs = ([tok_spec(D_A)] * 4 + [tok_spec(D_KV_B)] * 2
                 + [tail_spec(keep_a, D_A)] * 2 + [tail_spec(keep_b, D_KV_B)] * 2)
    return pl.pallas_call(
        functools.partial(_proj_body, tm=tm, keep_a=keep_a, keep_b=keep_b, row_split=row_split),
        grid=grid, in_specs=in_specs, out_specs=out_specs, out_shape=out_shape,
        compiler_params=_cparams(("arbitrary", "arbitrary")),
        name="proj",
    )(x3, g_mix, w_p, cos_t, sa_t, sb_t)


def _attend_a_quad(q, kw, vw, bias, hm_ref, thr):
    r = q.shape[0]
    qs = jnp.concatenate([q * hm_ref[h] for h in range(4)], axis=0)
    s = lax.dot_general(qs, kw, (((1,), (1,)), ((), ())), preferred_element_type=F32) + bias
    if thr is not None:
        col = lax.broadcasted_iota(jnp.int32, s.shape, 1)
        s = jnp.where(col >= thr, s, NEG_INF)
    m = jnp.max(s, axis=-1, keepdims=True)
    p = jnp.exp(s - m)
    denom = jnp.sum(p, axis=-1, keepdims=True)
    pv = jnp.dot(p.astype(BF16), vw, preferred_element_type=F32) * (1.0 / denom)
    lane_head = lax.broadcasted_iota(jnp.int32, (r, QUAD), 1) // HEAD_DIM
    o = pv[0:r]
    for h in range(1, 4):
        o = jnp.where(lane_head == h, pv[h * r:(h + 1) * r], o)
    return o


def _attend_b(q, kw, vw, sbias, live_ref, lm_ref, thr):
    r = q.shape[0]
    pieces = []
    for p in range(N_PAIRS):
        qp = q[:, p * LANES:(p + 1) * LANES]
        pieces.append(qp * lm_ref[0])
        pieces.append(qp * lm_ref[1])
    qs = jnp.concatenate(pieces, axis=0)
    kz = kw * live_ref[0]
    vaug = jnp.concatenate([vw * live_ref[0], live_ref[1]], axis=1)
    s = lax.dot_general(qs, kz, (((1,), (1,)), ((), ())), preferred_element_type=F32) + sbias
    if thr is not None:
        col = lax.broadcasted_iota(jnp.int32, s.shape, 1)
        s = jnp.where(col >= thr, s, NEG_INF)
    m = jnp.max(s, axis=-1, keepdims=True)
    p = jnp.exp(s - m)
    pvd = jnp.dot(p.astype(BF16), vaug, preferred_element_type=F32)
    pv = pvd[:, :LANES] * (1.0 / pvd[:, LANES:])
    lower = lax.broadcasted_iota(jnp.int32, (r, LANES), 1) < HEAD_DIM
    outs = [jnp.where(lower, pv[(2 * p) * r:(2 * p + 1) * r], pv[(2 * p + 1) * r:(2 * p + 2) * r])
            for p in range(N_PAIRS)]
    return jnp.concatenate(outs, axis=1)


def _group_norm_store(o_parts, g_ref, o_ref, rows, col0):
    width = sum(o.shape[1] for o in o_parts)
    ssq = sum(jnp.sum(o * o, axis=-1, keepdims=True) for o in o_parts)
    inv = lax.rsqrt(ssq * (1.0 / width) + EPS)
    off = 0
    for o in o_parts:
        w = o.shape[1]
        o_ref[rows, col0 + off:col0 + off + w] = ((o * inv) * g_ref[:, off:off + w]).astype(BF16)
        off += w


def _attn_prompt_body(qa_ref, kap_ref, kac_ref, vap_ref, vac_ref,
                      qb_ref, kbp_ref, kbc_ref, vbp_ref, vbc_ref,
                      bias_ref, hm_ref, lm_ref, sbias_ref, live_ref, ga_ref, gb_ref,
                      o_ref, kwin, vwin, kbwin, vbwin):
    i = pl.program_id(1)
    for win, prev, cur in ((kwin, kap_ref, kac_ref), (vwin, vap_ref, vac_ref),
                           (kbwin, kbp_ref, kbc_ref), (vbwin, vbp_ref, vbc_ref)):
        win[0:TQ] = prev[...]
        win[TQ:2 * TQ] = cur[...]
        win[2 * TQ:2 * TQ + CHUNK] = jnp.zeros((CHUNK, win.shape[1]), BF16)

    def run(first_step):
        def body(c, carry):
            r0 = pl.multiple_of(c * CHUNK, CHUNK)
            rows = pl.ds(r0, CHUNK)
            thr_a = (A_PREV_CHUNKS - c) * CHUNK if first_step else None
            thr_b = jnp.maximum(B_PREV_CHUNKS - c, 0) * CHUNK if first_step else None
            parts = []
            for quad in range(N_QUADS):
                cols = slice(quad * QUAD, (quad + 1) * QUAD)
                parts.append(_attend_a_quad(
                    qa_ref[rows, cols], kwin[pl.ds(r0, A_WIN_PAD), cols], vwin[pl.ds(r0, A_WIN_PAD), cols],
                    bias_ref[quad], hm_ref, thr_a))
            _group_norm_store(parts, ga_ref, o_ref, rows, 0)
            b0 = pl.multiple_of(c * CHUNK + (TQ - B_WINDOW), CHUNK)
            ob = _attend_b(qb_ref[rows, :], kbwin[pl.ds(b0, B_WIN_PAD), :], vbwin[pl.ds(b0, B_WIN_PAD), :],
                           sbias_ref[...], live_ref, lm_ref, thr_b)
            _group_norm_store([ob], gb_ref, o_ref, rows, D_A)
            return carry
        lax.fori_loop(0, TQ // CHUNK, body, 0, unroll=4)

    @pl.when(i == 0)
    def _():
        run(True)

    @pl.when(i > 0)
    def _():
        run(False)


def _attn_prompt_call(qa, ka, va, qb, kb, vb, bias, hm, lm, sbias, live, g_a, g_b):
    nb, s, _ = qa.shape
    assert s % TQ == 0
    grid = (nb, s // TQ)
    cur = lambda width: pl.BlockSpec((None, TQ, width), lambda b, i: (b, i, 0))
    prev = lambda width: pl.BlockSpec((None, TQ, width), lambda b, i: (b, jnp.maximum(i - 1, 0), 0))
    const = lambda shape: pl.BlockSpec(shape, lambda b, i: (0,) * len(shape))
    in_specs = [
        cur(D_A), prev(D_A), cur(D_A), prev(D_A), cur(D_A),
        cur(D_B), prev(D_KV_B), cur(D_KV_B), prev(D_KV_B), cur(D_KV_B),
        const(bias.shape), const(hm.shape), const(lm.shape), const(sbias.shape), const(live.shape),
        const(g_a.shape), const(g_b.shape),
    ]
    win_rows = 2 * TQ + CHUNK
    return pl.pallas_call(
        _attn_prompt_body,
        grid=grid, in_specs=in_specs,
        out_specs=pl.BlockSpec((None, TQ, D_A + D_B), lambda b, i: (b, i, 0)),
        out_shape=jax.ShapeDtypeStruct((nb, s, D_A + D_B), BF16),
        scratch_shapes=[pltpu.VMEM((win_rows, D_A), BF16), pltpu.VMEM((win_rows, D_A), BF16),
                        pltpu.VMEM((win_rows, D_KV_B), BF16), pltpu.VMEM((win_rows, D_KV_B), BF16)],
        compiler_params=_cparams(("arbitrary", "arbitrary")),
        name="attn_prompt",
    )(qa, ka, ka, va, va, qb, kb, kb, vb, vb, bias, hm, lm, sbias, live, g_a, g_b)


def _attn_sample_body(qa_ref, cka_ref, cva_ref, nka_ref, nva_ref,
                      qb_ref, ckb_ref, cvb_ref, nkb_ref, nvb_ref,
                      bias_ref, hm_ref, lm_ref, sbias_ref, live_ref, ga_ref, gb_ref, o_ref, *, pad_a, pad_b):
    rows = slice(None)
    parts = []
    for quad in range(N_QUADS):
        cols = slice(quad * QUAD, (quad + 1) * QUAD)
        kw = jnp.concatenate([cka_ref[:, cols].astype(BF16), nka_ref[:, cols],
                              jnp.zeros((pad_a, QUAD), BF16)], axis=0)
        vw = jnp.concatenate([cva_ref[:, cols].astype(BF16), nva_ref[:, cols],
                              jnp.zeros((pad_a, QUAD), BF16)], axis=0)
        parts.append(_attend_a_quad(qa_ref[:, cols], kw, vw, bias_ref[quad], hm_ref, None))
    _group_norm_store(parts, ga_ref, o_ref, rows, 0)
    kbw = jnp.concatenate([ckb_ref[...].astype(BF16), nkb_ref[...], jnp.zeros((pad_b, D_KV_B), BF16)], axis=0)
    vbw = jnp.concatenate([cvb_ref[...].astype(BF16), nvb_ref[...], jnp.zeros((pad_b, D_KV_B), BF16)], axis=0)
    ob = _attend_b(qb_ref[...], kbw, vbw, sbias_ref[...], live_ref, lm_ref, None)
    _group_norm_store([ob], gb_ref, o_ref, rows, D_A)


def _attn_sample_call(qa, cka, cva, nka, nva, qb, ckb, cvb, nkb, nvb, bias, hm, lm, sbias, live, g_a, g_b,
                      *, pad_a, pad_b):
    nb, s, _ = qa.shape
    blk = lambda a: pl.BlockSpec((None,) + a.shape[1:], lambda b: (b, 0, 0))
    const = lambda a: pl.BlockSpec(a.shape, lambda b: (0,) * a.ndim)
    per_batch = (qa, cka, cva, nka, nva, qb, ckb, cvb, nkb, nvb)
    consts = (bias, hm, lm, sbias, live, g_a, g_b)
    return pl.pallas_call(
        functools.partial(_attn_sample_body, pad_a=pad_a, pad_b=pad_b),
        grid=(nb,),
        in_specs=[blk(a) for a in per_batch] + [const(a) for a in consts],
        out_specs=pl.BlockSpec((None, s, D_A + D_B), lambda b: (b, 0, 0)),
        out_shape=jax.ShapeDtypeStruct((nb, s, D_A + D_B), BF16),
        compiler_params=_cparams(("arbitrary",)),
        name="attn_sample",
    )(*per_batch, *consts)


def _ffn_body(x_ref, o_ref, wo_ref, gf_ref, wg_ref, wu_ref, wd_ref, gfin_ref, y_ref, h_scr):
    k = pl.program_id(1)

    @pl.when(k == 0)
    def _():
        x1 = x_ref[...] + jnp.dot(o_ref[...], wo_ref[...], preferred_element_type=F32)
        y_ref[...] = x1
        ms = jnp.mean(x1 * x1, axis=-1, keepdims=True)
        h_scr[...] = ((x1 * lax.rsqrt(ms + EPS)) * gf_ref[...]).astype(BF16)

    h = h_scr[...]
    g = jnp.dot(h, wg_ref[...], preferred_element_type=F32)
    u = jnp.dot(h, wu_ref[...], preferred_element_type=F32)
    act = (g * jax.nn.sigmoid(g)) * u
    y_ref[...] += jnp.dot(act.astype(BF16), wd_ref[...], preferred_element_type=F32)

    @pl.when(k == pl.num_programs(1) - 1)
    def _():
        y = y_ref[...]
        ms = jnp.mean(y * y, axis=-1, keepdims=True)
        y_ref[...] = (y * lax.rsqrt(ms + EPS)) * gfin_ref[...]


def _ffn_call(x2, o2, w_out, g_ffn, w_g, w_u, w_d, g_fin, *, tm):
    n, d = x2.shape
    assert n % tm == 0
    grid = (n // tm, D_FF // TF)
    in_specs = [
        pl.BlockSpec((tm, d), lambda i, k: (i, 0)),
        pl.BlockSpec((tm, D_A + D_B), lambda i, k: (i, 0)),
        pl.BlockSpec((D_A + D_B, d), lambda i, k: (0, 0), pipeline_mode=pl.Buffered(1)),
        pl.BlockSpec((1, d), lambda i, k: (0, 0)),
        pl.BlockSpec((d, TF), lambda i, k: (0, k)),
        pl.BlockSpec((d, TF), lambda i, k: (0, k)),
        pl.BlockSpec((TF, d), lambda i, k: (k, 0)),
        pl.BlockSpec((1, d), lambda i, k: (0, 0)),
    ]
    return pl.pallas_call(
        _ffn_body,
        grid=grid, in_specs=in_specs,
        out_specs=pl.BlockSpec((tm, d), lambda i, k: (i, 0)),
        out_shape=jax.ShapeDtypeStruct((n, d), F32),
        scratch_shapes=[pltpu.VMEM((tm, d), BF16)],
        compiler_params=_cparams(("arbitrary", "arbitrary")),
        name="ffn",
    )(x2, o2, w_out, g_ffn, w_g, w_u, w_d, g_fin)


def _rope_tables(pos):
    half = ROPE_DIM // 2
    n = pos.shape[0]
    inv_freq = ROPE_THETA ** (-jnp.arange(half, dtype=F32) * 2.0 / ROPE_DIM)
    ang = pos.astype(F32)[:, None] * inv_freq[None, :]
    cos = jnp.cos(ang)
    sin = jnp.sin(ang)
    rest = HEAD_DIM - ROPE_DIM
    cos_h = jnp.concatenate([cos, cos, jnp.ones((n, rest), F32)], axis=1)
    sa_h = jnp.concatenate([-sin, jnp.zeros((n, HEAD_DIM - half), F32)], axis=1)
    sb_h = jnp.concatenate([jnp.zeros((n, half), F32), sin, jnp.zeros((n, rest), F32)], axis=1)
    rep = LANES // HEAD_DIM
    return tuple(jnp.tile(t, (1, rep)) for t in (cos_h, sa_h, sb_h))


def _bias_table(rel_table, nq, n_past, nk, width):
    u_max = nq - 1 + n_past
    span = nq + nk - 1
    idx = np.roll(np.clip(u_max - np.arange(span), -REL_CLIP, REL_CLIP) + REL_CLIP, -(nq - 1))
    e_r = rel_table.astype(F32)[:, idx]
    h = e_r.shape[0]
    b = jnp.tile(e_r, (1, nq))[:, :nq * (span - 1)].reshape(h, nq, span - 1)[:, :, :nk]
    b = jnp.pad(b, ((0, 0), (0, 0), (0, width - nk)), constant_values=NEG_INF)
    return b.reshape(N_QUADS, 4 * nq, width)


def _sink_bias(sinks, r, live, width):
    assert live < width
    inter = sinks.astype(F32).reshape(N_KV_B, GQA_R).T.reshape(N_HEADS_B)
    rows = jnp.repeat(inter, r)[:, None]
    col = jnp.arange(width)[None, :]
    return jnp.where(col < live, 0.0, jnp.where(col == live, rows, NEG_INF)).astype(F32)


def _live_rows(live, width):
    row = jnp.arange(width)[:, None]
    return jnp.stack([jnp.broadcast_to(row < live, (width, LANES)),
                      jnp.broadcast_to(row <= live, (width, LANES))]).astype(BF16)


def kernel(x_prompt, x_sample, cache_a_k, cache_a_v, cache_b_k, cache_b_v, w_in, norm_mix, rel_table, sinks,
           norm_grp_a, norm_grp_b, w_out, norm_ffn, w_gate, w_up, w_down, norm_final):
    assert w_in.shape[0] == 1, "single-layer problem"
    nb, s, d = x_prompt.shape
    sb, ss, _ = x_sample.shape
    keep_a = min(A_REACH, s)
    keep_b = min(B_WINDOW, s)
    scale = HEAD_DIM ** -0.5

    w = w_in[0]
    w_qb = w[:, 3 * D_A:3 * D_A + D_B].reshape(d, N_KV_B, GQA_R, HEAD_DIM).transpose(0, 2, 1, 3).reshape(d, D_B)
    w_p = jnp.concatenate([w[:, :D_A] * scale, w[:, D_A:3 * D_A], w_qb * scale, w[:, 3 * D_A + D_B:]],
                          axis=1).astype(BF16)
    g_mix = norm_mix[0][None, :]
    g_a = norm_grp_a[0][None, :]
    g_b = norm_grp_b[0].reshape(N_KV_B, GQA_R, HEAD_DIM).transpose(1, 0, 2).reshape(1, D_B)
    wo = w_out[0]
    wo_b = wo[D_A:].reshape(N_KV_B, GQA_R, HEAD_DIM, d).transpose(1, 0, 2, 3).reshape(D_B, d)
    wo_p = jnp.concatenate([wo[:D_A], wo_b], axis=0).astype(BF16)
    g_ffn = norm_ffn[0][None, :]
    w_g = w_gate[0].astype(BF16)
    w_u = w_up[0].astype(BF16)
    w_d = w_down[0].astype(BF16)
    g_fin = norm_final[None, :]

    def head_masks(width, rows):
        lane = jnp.arange(width) // HEAD_DIM
        m = (lane[None, :] == jnp.arange(width // HEAD_DIM)[:, None]).astype(BF16)
        return jnp.broadcast_to(m[:, None, :], (width // HEAD_DIM, rows, width))

    hm, lm = head_masks(QUAD, CHUNK), head_masks(LANES, CHUNK)
    hm_s, lm_s = head_masks(QUAD, ss), head_masks(LANES, ss)

    tables = _rope_tables(jnp.arange(s, dtype=jnp.int32))
    qa, ka, va, qb, kb, vb, ka_t, va_t, kb_t, vb_t = _proj_call(
        x_prompt, g_mix, w_p, *tables, tm=TQ, keep_a=keep_a, keep_b=keep_b, row_split=2)
    bias_p = _bias_table(rel_table[0], CHUNK, A_REACH, A_WIN, A_WIN_PAD)
    o_p = _attn_prompt_call(qa, ka, va, qb, kb, vb, bias_p, hm, lm,
                            _sink_bias(sinks[0], CHUNK, B_WIN, B_WIN_PAD), _live_rows(B_WIN, B_WIN_PAD), g_a, g_b)
    y_p = _ffn_call(x_prompt.reshape(nb * s, d), o_p.reshape(nb * s, D_A + D_B), wo_p, g_ffn, w_g, w_u, w_d, g_fin,
                    tm=TQ).reshape(nb, s, d)

    ns = sb * ss
    pos_s = jnp.tile(PAST_LEN + jnp.arange(ss, dtype=jnp.int32), sb)
    tables_s = _rope_tables(pos_s)
    sqa, ska, sva, sqb, skb, svb, ska_t, sva_t, skb_t, svb_t = _proj_call(
        x_sample.reshape(1, ns, d), g_mix, w_p, *tables_s, tm=ns, keep_a=ns, keep_b=ns, row_split=1)
    ca_len = cache_a_k.shape[2]
    cb_len = cache_b_k.shape[2]
    live_a = ca_len + ss
    live_b = cb_len + ss
    wa = -(-live_a // LANES) * LANES
    wb = -(-live_b // LANES) * LANES
    bias_s = _bias_table(rel_table[0], ss, ca_len, live_a, wa)
    per_b = lambda a, width: a.reshape(sb, -1, width)
    o_s = _attn_sample_call(
        per_b(sqa, D_A), per_b(cache_a_k[0], D_A), per_b(cache_a_v[0], D_A), per_b(ska, D_A), per_b(sva, D_A),
        per_b(sqb, D_B), per_b(cache_b_k[0], D_KV_B), per_b(cache_b_v[0], D_KV_B), per_b(skb, D_KV_B),
        per_b(svb, D_KV_B),
        bias_s, hm_s, lm_s, _sink_bias(sinks[0], ss, live_b, wb), _live_rows(live_b, wb), g_a, g_b,
        pad_a=wa - live_a, pad_b=wb - live_b)
    y_s = _ffn_call(x_sample.reshape(ns, d), o_s.reshape(ns, D_A + D_B), wo_p, g_ffn, w_g, w_u, w_d, g_fin,
                    tm=ns).reshape(sb, ss, d)

    heads_a = lambda t, nbat: t.reshape(1, nbat, -1, N_HEADS_A, HEAD_DIM)
    heads_b = lambda t, nbat: t.reshape(1, nbat, -1, N_KV_B, HEAD_DIM)
    return (y_p, y_s,
            heads_a(ka_t, nb), heads_a(va_t, nb), heads_b(kb_t, nb), heads_b(vb_t, nb),
            heads_a(ska_t, sb), heads_a(sva_t, sb), heads_b(skb_t, sb), heads_b(svb_t, sb))
```

```python
import functools

import numpy as np
import jax
import jax.numpy as jnp
from jax import lax
from jax.experimental import pallas as pl
from jax.experimental.pallas import tpu as pltpu

D_MODEL = 2048
CHUNK = 64
HEAD_DIM = 64
N_HEADS_A = 16
N_HEADS_B = 16
N_KV_B = 2
GQA_R = N_HEADS_B // N_KV_B
D_A = N_HEADS_A * HEAD_DIM
D_B = N_HEADS_B * HEAD_DIM
D_KV_B = N_KV_B * HEAD_DIM
D_IN = 3 * D_A + D_B + 2 * D_KV_B
A_PREV_CHUNKS = 8
A_REACH = A_PREV_CHUNKS * CHUNK
REL_CLIP = 128
B_WINDOW = 128
B_PREV_CHUNKS = B_WINDOW // CHUNK
ROPE_THETA = 500000.0
ROPE_DIM = HEAD_DIM // 4
D_FF = 5632
NEG_INF = -1e30
EPS = 1e-6
PAST_LEN = 2048

LANES = 128
QUAD = 4 * HEAD_DIM
N_QUADS = N_HEADS_A // 4
N_PAIRS = N_HEADS_B // 2
TQ = 512
A_WIN = (A_PREV_CHUNKS + 1) * CHUNK
A_WIN_PAD = 640
B_WIN = (B_PREV_CHUNKS + 1) * CHUNK
B_WIN_PAD = 256
TF = 512
VMEM_LIMIT = 56 * 1024 * 1024

F32 = jnp.float32
BF16 = jnp.bfloat16


def _cparams(sem):
    return pltpu.CompilerParams(dimension_semantics=sem, vmem_limit_bytes=VMEM_LIMIT)


def _proj_body(x_ref, g_ref, w_ref, cos_ref, sa_ref, sb_ref,
               qa_ref, ka_ref, va_ref, qb_ref, kb_ref, vb_ref,
               kat_ref, vat_ref, kbt_ref, vbt_ref, *, tm, keep_a, keep_b, row_split):
    hm = tm // row_split
    for r0 in range(0, tm, hm):
        rows = slice(r0, r0 + hm)
        x = x_ref[rows, :]
        ms = jnp.mean(x * x, axis=-1, keepdims=True)
        xn = ((x * lax.rsqrt(ms + EPS)) * g_ref[...]).astype(BF16)

        def proj(lo, hi):
            return jnp.dot(xn, w_ref[:, lo:hi], preferred_element_type=F32)

        cos = cos_ref[rows, :]
        sa = sa_ref[rows, :]
        sb = sb_ref[rows, :]

        def rope(y):
            return y * cos + pltpu.roll(y, LANES - ROPE_DIM // 2, 1) * sa + pltpu.roll(y, ROPE_DIM // 2, 1) * sb

        def tail(dst_ref, val, keep):
            lo, hi = max(r0, tm - keep), r0 + hm
            if lo < hi:
                dst_ref[lo - (tm - keep):hi - (tm - keep), :] = val[lo - r0:hi - r0, :]

        qa_ref[rows, :] = proj(0, D_A).astype(BF16)

        ka = proj(D_A, 2 * D_A)
        ka_ref[rows, :] = ka.astype(BF16)
        tail(kat_ref, ka, keep_a)

        va = proj(2 * D_A, 3 * D_A)
        va_ref[rows, :] = va.astype(BF16)
        tail(vat_ref, va, keep_a)

        qb = proj(3 * D_A, 3 * D_A + D_B)
        for j in range(D_B // LANES):
            qb_ref[rows, j * LANES:(j + 1) * LANES] = rope(qb[:, j * LANES:(j + 1) * LANES]).astype(BF16)

        kvb = proj(3 * D_A + D_B, D_IN)
        kb = rope(kvb[:, :D_KV_B])
        vb = kvb[:, D_KV_B:]
        kb_ref[rows, :] = kb.astype(BF16)
        vb_ref[rows, :] = vb.astype(BF16)
        tail(kbt_ref, kb, keep_b)
        tail(vbt_ref, vb, keep_b)


def _proj_call(x3, g_mix, w_p, cos_t, sa_t, sb_t, *, tm, keep_a, keep_b, row_split):
    nb, s, d = x3.shape
    assert s % tm == 0 and keep_a <= tm and keep_b <= tm and tm % row_split == 0
    grid = (nb, s // tm)

    def tok(width):
        return jax.ShapeDtypeStruct((nb, s, width), BF16)

    out_shape = ([tok(D_A)] * 4 + [tok(D_KV_B)] * 2
                 + [jax.ShapeDtypeStruct((nb, keep_a, D_A), F32)] * 2
                 + [jax.ShapeDtypeStruct((nb, keep_b, D_KV_B), F32)] * 2)
    tok_spec = lambda width: pl.BlockSpec((None, tm, width), lambda b, i: (b, i, 0))
    tail_spec = lambda rows, width: pl.BlockSpec((None, rows, width), lambda b, i: (b, 0, 0))
    tab_spec = pl.BlockSpec((tm, LANES), lambda b, i: (i, 0))
    in_specs = [
        tok_spec(d),
        pl.BlockSpec((1, d), lambda b, i: (0, 0)),
        pl.BlockSpec((d, D_IN), lambda b, i: (0, 0), pipeline_mode=pl.Buffered(1)),
        tab_spec, tab_spec, tab_spec,
    ]
    out_specs = ([tok_spec(D_A)] * 4 + [tok_spec(D_KV_B)] * 2
                 + [tail_spec(keep_a, D_A)] * 2 + [tail_spec(keep_b, D_KV_B)] * 2)
    return pl.pallas_call(
        functools.partial(_proj_body, tm=tm, keep_a=keep_a, keep_b=keep_b, row_split=row_split),
        grid=grid, in_specs=in_specs, out_specs=out_specs, out_shape=out_shape,
        compiler_params=_cparams(("arbitrary", "arbitrary")),
        name="proj",
    )(x3, g_mix, w_p, cos_t, sa_t, sb_t)


def _attend_a_quad(q, kw, vw, bias, hm_ref, thr):
    r = q.shape[0]
    qs = jnp.concatenate([q * hm_ref[h] for h in range(4)], axis=0)
    s = lax.dot_general(qs, kw, (((1,), (1,)), ((), ())), preferred_element_type=F32) + bias
    if thr is not None:
        col = lax.broadcasted_iota(jnp.int32, s.shape, 1)
        s = jnp.where(col >= thr, s, NEG_INF)
    m = jnp.max(s, axis=-1, keepdims=True)
    p = jnp.exp(s - m)
    denom = jnp.sum(p, axis=-1, keepdims=True)
    pv = jnp.dot(p.astype(BF16), vw, preferred_element_type=F32) * (1.0 / denom)
    lane_head = lax.broadcasted_iota(jnp.int32, (r, QUAD), 1) // HEAD_DIM
    o = pv[0:r]
    for h in range(1, 4):
        o = jnp.where(lane_head == h, pv[h * r:(h + 1) * r], o)
    return o


def _attend_b(q, kw, vw, sbias, live_ref, lm_ref, thr):
    r = q.shape[0]
    pieces = []
    for p in range(N_PAIRS):
        qp = q[:, p * LANES:(p + 1) * LANES]
        pieces.append(qp * lm_ref[0])
        pieces.append(qp * lm_ref[1])
    qs = jnp.concatenate(pieces, axis=0)
    kz = kw * live_ref[0]
    vaug = jnp.concatenate([vw * live_ref[0], live_ref[1]], axis=1)
    s = lax.dot_general(qs, kz, (((1,), (1,)), ((), ())), preferred_element_type=F32) + sbias
    if thr is not None:
        col = lax.broadcasted_iota(jnp.int32, s.shape, 1)
        s = jnp.where(col >= thr, s, NEG_INF)
    m = jnp.max(s, axis=-1, keepdims=True)
    p = jnp.exp(s - m)
    pvd = jnp.dot(p.astype(BF16), vaug, preferred_element_type=F32)
    pv = pvd[:, :LANES] * (1.0 / pvd[:, LANES:])
    lower = lax.broadcasted_iota(jnp.int32, (r, LANES), 1) < HEAD_DIM
    outs = [jnp.where(lower, pv[(2 * p) * r:(2 * p + 1) * r], pv[(2 * p + 1) * r:(2 * p + 2) * r])
            for p in range(N_PAIRS)]
    return jnp.concatenate(outs, axis=1)


def _group_norm_store(o_parts, g_ref, o_ref, rows, col0):
    width = sum(o.shape[1] for o in o_parts)
    ssq = sum(jnp.sum(o * o, axis=-1, keepdims=True) for o in o_parts)
    inv = lax.rsqrt(ssq * (1.0 / width) + EPS)
    off = 0
    for o in o_parts:
        w = o.shape[1]
        o_ref[rows, col0 + off:col0 + off + w] = ((o * inv) * g_ref[:, off:off + w]).astype(BF16)
        off += w


def _attn_prompt_body(qa_ref, kap_ref, kac_ref, vap_ref, vac_ref,
                      qb_ref, kbp_ref, kbc_ref, vbp_ref, vbc_ref,
                      bias_ref, hm_ref, lm_ref, sbias_ref, live_ref, ga_ref, gb_ref,
                      o_ref, kwin, vwin, kbwin, vbwin):
    i = pl.program_id(1)
    for win, prev, cur in ((kwin, kap_ref, kac_ref), (vwin, vap_ref, vac_ref),
                           (kbwin, kbp_ref, kbc_ref), (vbwin, vbp_ref, vbc_ref)):
        win[0:TQ] = prev[...]
        win[TQ:2 * TQ] = cur[...]
        win[2 * TQ:2 * TQ + CHUNK] = jnp.zeros((CHUNK, win.shape[1]), BF16)

    def run(first_step):
        def body(c, carry):
            r0 = pl.multiple_of(c * CHUNK, CHUNK)
            rows = pl.ds(r0, CHUNK)
            thr_a = (A_PREV_CHUNKS - c) * CHUNK if first_step else None
            thr_b = jnp.maximum(B_PREV_CHUNKS - c, 0) * CHUNK if first_step else None
            parts = []
            for quad in range(N_QUADS):
                cols = slice(quad * QUAD, (quad + 1) * QUAD)
                parts.append(_attend_a_quad(
                    qa_ref[rows, cols], kwin[pl.ds(r0, A_WIN_PAD), cols], vwin[pl.ds(r0, A_WIN_PAD), cols],
                    bias_ref[quad], hm_ref, thr_a))
            _group_norm_store(parts, ga_ref, o_ref, rows, 0)
            b0 = pl.multiple_of(c * CHUNK + (TQ - B_WINDOW), CHUNK)
            ob = _attend_b(qb_ref[rows, :], kbwin[pl.ds(b0, B_WIN_PAD), :], vbwin[pl.ds(b0, B_WIN_PAD), :],
                           sbias_ref[...], live_ref, lm_ref, thr_b)
            _group_norm_store([ob], gb_ref, o_ref, rows, D_A)
            return carry
        lax.fori_loop(0, TQ // CHUNK, body, 0, unroll=True)

    @pl.when(i == 0)
    def _():
        run(True)

    @pl.when(i > 0)
    def _():
        run(False)


def _attn_prompt_call(qa, ka, va, qb, kb, vb, bias, hm, lm, sbias, live, g_a, g_b):
    nb, s, _ = qa.shape
    assert s % TQ == 0
    grid = (nb, s // TQ)
    cur = lambda width: pl.BlockSpec((None, TQ, width), lambda b, i: (b, i, 0))
    prev = lambda width: pl.BlockSpec((None, TQ, width), lambda b, i: (b, jnp.maximum(i - 1, 0), 0))
    const = lambda shape: pl.BlockSpec(shape, lambda b, i: (0,) * len(shape))
    in_specs = [
        cur(D_A), prev(D_A), cur(D_A), prev(D_A), cur(D_A),
        cur(D_B), prev(D_KV_B), cur(D_KV_B), prev(D_KV_B), cur(D_KV_B),
        const(bias.shape), const(hm.shape), const(lm.shape), const(sbias.shape), const(live.shape),
        const(g_a.shape), const(g_b.shape),
    ]
    win_rows = 2 * TQ + CHUNK
    return pl.pallas_call(
        _attn_prompt_body,
        grid=grid, in_specs=in_specs,
        out_specs=pl.BlockSpec((None, TQ, D_A + D_B), lambda b, i: (b, i, 0)),
        out_shape=jax.ShapeDtypeStruct((nb, s, D_A + D_B), BF16),
        scratch_shapes=[pltpu.VMEM((win_rows, D_A), BF16), pltpu.VMEM((win_rows, D_A), BF16),
                        pltpu.VMEM((win_rows, D_KV_B), BF16), pltpu.VMEM((win_rows, D_KV_B), BF16)],
        compiler_params=_cparams(("arbitrary", "arbitrary")),
        name="attn_prompt",
    )(qa, ka, ka, va, va, qb, kb, kb, vb, vb, bias, hm, lm, sbias, live, g_a, g_b)


def _attn_sample_body(qa_ref, cka_ref, cva_ref, nka_ref, nva_ref,
                      qb_ref, ckb_ref, cvb_ref, nkb_ref, nvb_ref,
                      bias_ref, hm_ref, lm_ref, sbias_ref, live_ref, ga_ref, gb_ref, o_ref, *, pad_a, pad_b):
    rows = slice(None)
    parts = []
    for quad in range(N_QUADS):
        cols = slice(quad * QUAD, (quad + 1) * QUAD)
        kw = jnp.concatenate([cka_ref[:, cols].astype(BF16), nka_ref[:, cols],
                              jnp.zeros((pad_a, QUAD), BF16)], axis=0)
        vw = jnp.concatenate([cva_ref[:, cols].astype(BF16), nva_ref[:, cols],
                              jnp.zeros((pad_a, QUAD), BF16)], axis=0)
        parts.append(_attend_a_quad(qa_ref[:, cols], kw, vw, bias_ref[quad], hm_ref, None))
    _group_norm_store(parts, ga_ref, o_ref, rows, 0)
    kbw = jnp.concatenate([ckb_ref[...].astype(BF16), nkb_ref[...], jnp.zeros((pad_b, D_KV_B), BF16)], axis=0)
    vbw = jnp.concatenate([cvb_ref[...].astype(BF16), nvb_ref[...], jnp.zeros((pad_b, D_KV_B), BF16)], axis=0)
    ob = _attend_b(qb_ref[...], kbw, vbw, sbias_ref[...], live_ref, lm_ref, None)
    _group_norm_store([ob], gb_ref, o_ref, rows, D_A)


def _attn_sample_call(qa, cka, cva, nka, nva, qb, ckb, cvb, nkb, nvb, bias, hm, lm, sbias, live, g_a, g_b,
                      *, pad_a, pad_b):
    nb, s, _ = qa.shape
    blk = lambda a: pl.BlockSpec((None,) + a.shape[1:], lambda b: (b, 0, 0))
    const = lambda a: pl.BlockSpec(a.shape, lambda b: (0,) * a.ndim)
    per_batch = (qa, cka, cva, nka, nva, qb, ckb, cvb, nkb, nvb)
    consts = (bias, hm, lm, sbias, live, g_a, g_b)
    return pl.pallas_call(
        functools.partial(_attn_sample_body, pad_a=pad_a, pad_b=pad_b),
        grid=(nb,),
        in_specs=[blk(a) for a in per_batch] + [const(a) for a in consts],
        out_specs=pl.BlockSpec((None, s, D_A + D_B), lambda b: (b, 0, 0)),
        out_shape=jax.ShapeDtypeStruct((nb, s, D_A + D_B), BF16),
        compiler_params=_cparams(("arbitrary",)),
        name="attn_sample",
    )(*per_batch, *consts)


def _ffn_body(x_ref, o_ref, wo_ref, gf_ref, wg_ref, wu_ref, wd_ref, gfin_ref, y_ref, h_scr):
    k = pl.program_id(1)

    @pl.when(k == 0)
    def _():
        x1 = x_ref[...] + jnp.dot(o_ref[...], wo_ref[...], preferred_element_type=F32)
        y_ref[...] = x1
        ms = jnp.mean(x1 * x1, axis=-1, keepdims=True)
        h_scr[...] = ((x1 * lax.rsqrt(ms + EPS)) * gf_ref[...]).astype(BF16)

    h = h_scr[...]
    g = jnp.dot(h, wg_ref[...], preferred_element_type=F32)
    u = jnp.dot(h, wu_ref[...], preferred_element_type=F32)
    act = (g * jax.nn.sigmoid(g)) * u
    y_ref[...] += jnp.dot(act.astype(BF16), wd_ref[...], preferred_element_type=F32)

    @pl.when(k == pl.num_programs(1) - 1)
    def _():
        y = y_ref[...]
        ms = jnp.mean(y * y, axis=-1, keepdims=True)
        y_ref[...] = (y * lax.rsqrt(ms + EPS)) * gfin_ref[...]


def _ffn_call(x2, o2, w_out, g_ffn, w_g, w_u, w_d, g_fin, *, tm):
    n, d = x2.shape
    assert n % tm == 0
    grid = (n // tm, D_FF // TF)
    in_specs = [
        pl.BlockSpec((tm, d), lambda i, k: (i, 0)),
        pl.BlockSpec((tm, D_A + D_B), lambda i, k: (i, 0)),
        pl.BlockSpec((D_A + D_B, d), lambda i, k: (0, 0), pipeline_mode=pl.Buffered(1)),
        pl.BlockSpec((1, d), lambda i, k: (0, 0)),
        pl.BlockSpec((d, TF), lambda i, k: (0, k)),
        pl.BlockSpec((d, TF), lambda i, k: (0, k)),
        pl.BlockSpec((TF, d), lambda i, k: (k, 0)),
        pl.BlockSpec((1, d), lambda i, k: (0, 0)),
    ]
    return pl.pallas_call(
        _ffn_body,
        grid=grid, in_specs=in_specs,
        out_specs=pl.BlockSpec((tm, d), lambda i, k: (i, 0)),
        out_shape=jax.ShapeDtypeStruct((n, d), F32),
        scratch_shapes=[pltpu.VMEM((tm, d), BF16)],
        compiler_params=_cparams(("arbitrary", "arbitrary")),
        name="ffn",
    )(x2, o2, w_out, g_ffn, w_g, w_u, w_d, g_fin)


def _rope_tables(pos):
    half = ROPE_DIM // 2
    n = pos.shape[0]
    inv_freq = ROPE_THETA ** (-jnp.arange(half, dtype=F32) * 2.0 / ROPE_DIM)
    ang = pos.astype(F32)[:, None] * inv_freq[None, :]
    cos = jnp.cos(ang)
    sin = jnp.sin(ang)
    rest = HEAD_DIM - ROPE_DIM
    cos_h = jnp.concatenate([cos, cos, jnp.ones((n, rest), F32)], axis=1)
    sa_h = jnp.concatenate([-sin, jnp.zeros((n, HEAD_DIM - half), F32)], axis=1)
    sb_h = jnp.concatenate([jnp.zeros((n, half), F32), sin, jnp.zeros((n, rest), F32)], axis=1)
    rep = LANES // HEAD_DIM
    return tuple(jnp.tile(t, (1, rep)) for t in (cos_h, sa_h, sb_h))


def _bias_table(rel_table, nq, n_past, nk, width):
    u_max = nq - 1 + n_past
    span = nq + nk - 1
    idx = np.roll(np.clip(u_max - np.arange(span), -REL_CLIP, REL_CLIP) + REL_CLIP, -(nq - 1))
    e_r = rel_table.astype(F32)[:, idx]
    h = e_r.shape[0]
    b = jnp.tile(e_r, (1, nq))[:, :nq * (span - 1)].reshape(h, nq, span - 1)[:, :, :nk]
    b = jnp.pad(b, ((0, 0), (0, 0), (0, width - nk)), constant_values=NEG_INF)
    return b.reshape(N_QUADS, 4 * nq, width)


def _sink_bias(sinks, r, live, width):
    assert live < width
    inter = sinks.astype(F32).reshape(N_KV_B, GQA_R).T.reshape(N_HEADS_B)
    rows = jnp.repeat(inter, r)[:, None]
    col = jnp.arange(width)[None, :]
    return jnp.where(col < live, 0.0, jnp.where(col == live, rows, NEG_INF)).astype(F32)


def _live_rows(live, width):
    row = jnp.arange(width)[:, None]
    return jnp.stack([jnp.broadcast_to(row < live, (width, LANES)),
                      jnp.broadcast_to(row <= live, (width, LANES))]).astype(BF16)


def kernel(x_prompt, x_sample, cache_a_k, cache_a_v, cache_b_k, cache_b_v, w_in, norm_mix, rel_table, sinks,
           norm_grp_a, norm_grp_b, w_out, norm_ffn, w_gate, w_up, w_down, norm_final):
    assert w_in.shape[0] == 1, "single-layer problem"
    nb, s, d = x_prompt.shape
    sb, ss, _ = x_sample.shape
    keep_a = min(A_REACH, s)
    keep_b = min(B_WINDOW, s)
    scale = HEAD_DIM ** -0.5

    w = w_in[0]
    w_qb = w[:, 3 * D_A:3 * D_A + D_B].reshape(d, N_KV_B, GQA_R, HEAD_DIM).transpose(0, 2, 1, 3).reshape(d, D_B)
    w_p = jnp.concatenate([w[:, :D_A] * scale, w[:, D_A:3 * D_A], w_qb * scale, w[:, 3 * D_A + D_B:]],
                          axis=1).astype(BF16)
    g_mix = norm_mix[0][None, :]
    g_a = norm_grp_a[0][None, :]
    g_b = norm_grp_b[0].reshape(N_KV_B, GQA_R, HEAD_DIM).transpose(1, 0, 2).reshape(1, D_B)
    wo = w_out[0]
    wo_b = wo[D_A:].reshape(N_KV_B, GQA_R, HEAD_DIM, d).transpose(1, 0, 2, 3).reshape(D_B, d)
    wo_p = jnp.concatenate([wo[:D_A], wo_b], axis=0).astype(BF16)
    g_ffn = norm_ffn[0][None, :]
    w_g = w_gate[0].astype(BF16)
    w_u = w_up[0].astype(BF16)
    w_d = w_down[0].astype(BF16)
    g_fin = norm_final[None, :]

    def head_masks(width, rows):
        lane = jnp.arange(width) // HEAD_DIM
        m = (lane[None, :] == jnp.arange(width // HEAD_DIM)[:, None]).astype(BF16)
        return jnp.broadcast_to(m[:, None, :], (width // HEAD_DIM, rows, width))

    hm, lm = head_masks(QUAD, CHUNK), head_masks(LANES, CHUNK)
    hm_s, lm_s = head_masks(QUAD, ss), head_masks(LANES, ss)

    tables = _rope_tables(jnp.arange(s, dtype=jnp.int32))
    qa, ka, va, qb, kb, vb, ka_t, va_t, kb_t, vb_t = _proj_call(
        x_prompt, g_mix, w_p, *tables, tm=TQ, keep_a=keep_a, keep_b=keep_b, row_split=2)
    bias_p = _bias_table(rel_table[0], CHUNK, A_REACH, A_WIN, A_WIN_PAD)
    o_p = _attn_prompt_call(qa, ka, va, qb, kb, vb, bias_p, hm, lm,
                            _sink_bias(sinks[0], CHUNK, B_WIN, B_WIN_PAD), _live_rows(B_WIN, B_WIN_PAD), g_a, g_b)
    y_p = _ffn_call(x_prompt.reshape(nb * s, d), o_p.reshape(nb * s, D_A + D_B), wo_p, g_ffn, w_g, w_u, w_d, g_fin,
                    tm=TQ).reshape(nb, s, d)

    ns = sb * ss
    pos_s = jnp.tile(PAST_LEN + jnp.arange(ss, dtype=jnp.int32), sb)
    tables_s = _rope_tables(pos_s)
    sqa, ska, sva, sqb, skb, svb, ska_t, sva_t, skb_t, svb_t = _proj_call(
        x_sample.reshape(1, ns, d), g_mix, w_p, *tables_s, tm=ns, keep_a=ns, keep_b=ns, row_split=1)
    ca_len = cache_a_k.shape[2]
    cb_len = cache_b_k.shape[2]
    live_a = ca_len + ss
    live_b = cb_len + ss
    wa = -(-live_a // LANES) * LANES
    wb = -(-live_b // LANES) * LANES
    bias_s = _bias_table(rel_table[0], ss, ca_len, live_a, wa)
    per_b = lambda a, width: a.reshape(sb, -1, width)
    o_s = _attn_sample_call(
        per_b(sqa, D_A), per_b(cache_a_k[0], D_A), per_b(cache_a_v[0], D_A), per_b(ska, D_A), per_b(sva, D_A),
        per_b(sqb, D_B), per_b(cache_b_k[0], D_KV_B), per_b(cache_b_v[0], D_KV_B), per_b(skb, D_KV_B),
        per_b(svb, D_KV_B),
        bias_s, hm_s, lm_s, _sink_bias(sinks[0], ss, live_b, wb), _live_rows(live_b, wb), g_a, g_b,
        pad_a=wa - live_a, pad_b=wb - live_b)
    y_s = _ffn_call(x_sample.reshape(ns, d), o_s.reshape(ns, D_A + D_B), wo_p, g_ffn, w_g, w_u, w_d, g_fin,
                    tm=ns).reshape(sb, ss, d)

    heads_a = lambda t, nbat: t.reshape(1, nbat, -1, N_HEADS_A, HEAD_DIM)
    heads_b = lambda t, nbat: t.reshape(1, nbat, -1, N_KV_B, HEAD_DIM)
    return (y_p, y_s,
            heads_a(ka_t, nb), heads_a(va_t, nb), heads_b(kb_t, nb), heads_b(vb_t, nb),
            heads_a(ska_t, sb), heads_a(sva_t, sb), heads_b(skb_t, sb), heads_b(svb_t, sb))
```
